```python
import jax
import jax.numpy as jnp
from jax import lax
import numpy as np


D_MODEL = 1024
BATCH = 4
SEQ = 4096
DEPTH = 2

GRID_W = 64
CTX_LEN = 256
HEAD_DIM = 64
NA_WIDTH = D_MODEL // 2
NA_HEADS = NA_WIDTH // HEAD_DIM
NA_ROWS = 8
NA_COLS = 16
CONV_CH = D_MODEL // 4
CONV_K = 3
FOURIER_WIDTH = D_MODEL - NA_WIDTH - CONV_CH
FOURIER_GROUPS = 4
FOURIER_GROUP_DIM = FOURIER_WIDTH // FOURIER_GROUPS
IN_WIDTH = 3 * NA_WIDTH + 3 * CONV_CH + FOURIER_WIDTH
N_EXPERTS = 16
CAPACITY_FACTOR = 2
EXPERT_HIDDEN = 256 * (-(-8 * D_MODEL // (3 * 256)))
N_MOD = 6
RMS_EPS = 1e-6
NEG_INF = -1e30

kernel_name = 'hybrid_na_conv_fourier_ec_block'


def rms_norm(x, g):
    xf = x.astype(jnp.float32)
    y = xf * lax.rsqrt(jnp.mean(xf * xf, axis=-1, keepdims=True) + RMS_EPS)
    return (y * g.astype(jnp.float32)).astype(x.dtype)


def group_rms(t):
    tf = t.astype(jnp.float32)
    return tf * lax.rsqrt(jnp.mean(tf * tf, axis=-1, keepdims=True) + RMS_EPS)


def modulate(h, shift, scale):
    return h * (1 + scale) + shift


def split_proj(p):
    b, n, _ = p.shape
    heads = lambda t: t.reshape(b, n, NA_HEADS, HEAD_DIM)
    q = heads(p[..., 0:NA_WIDTH])
    k = heads(p[..., NA_WIDTH:2 * NA_WIDTH])
    v = heads(p[..., 2 * NA_WIDTH:3 * NA_WIDTH])
    o = 3 * NA_WIDTH
    bg = p[..., o:o + CONV_CH]
    cg = p[..., o + CONV_CH:o + 2 * CONV_CH]
    u = p[..., o + 2 * CONV_CH:o + 3 * CONV_CH]
    f = p[..., o + 3 * CONV_CH:]
    return q, k, v, bg, cg, u, f


def neighbourhood_attention(q, k, v, k_ctx, v_ctx, rpb):
    b, n, h, dh = q.shape
    rows = n // GRID_W
    wr = min(NA_ROWS, rows)
    qg = q.reshape(b, rows, GRID_W, h, dh)
    kg = k.reshape(b, rows, GRID_W, h, dh)
    vg = v.reshape(b, rows, GRID_W, h, dh)
    r = jnp.arange(rows)
    row_start = jnp.clip(r - wr // 2, 0, rows - wr)
    row_idx = row_start[:, None] + jnp.arange(wr)[None, :]
    k_rows = kg[:, row_idx]
    v_rows = vg[:, row_idx]
    cols = jnp.arange(GRID_W)
    col_start = jnp.clip(cols - NA_COLS // 2, 0, GRID_W - NA_COLS)
    col_in = (cols[None, :] >= col_start[:, None]) & (cols[None, :] < col_start[:, None] + NA_COLS)
    dr = row_idx - r[:, None] + (NA_ROWS - 1)
    dc = jnp.clip(cols[None, :] - cols[:, None], -(NA_COLS - 1), NA_COLS - 1) + (NA_COLS - 1)
    bias = rpb[:, dr[:, None, :, None], dc[None, :, None, :]]
    scale = HEAD_DIM ** -0.5
    s_win = jnp.einsum('brqhd,brikhd->bhrqik', qg, k_rows, preferred_element_type=jnp.float32) * scale + bias
    s_win = jnp.where(col_in[:, None, :], s_win, NEG_INF)
    s_ctx = jnp.einsum('brqhd,bjhd->bhrqj', qg, k_ctx, preferred_element_type=jnp.float32) * scale
    n_win = wr * GRID_W
    s = jnp.concatenate([s_win.reshape(b, h, rows, GRID_W, n_win), s_ctx], axis=-1)
    p = jax.nn.softmax(s, axis=-1).astype(v.dtype)
    p_win = p[..., :n_win].reshape(b, h, rows, GRID_W, wr, GRID_W)
    p_ctx = p[..., n_win:]
    out = jnp.einsum('bhrqik,brikhd->brqhd', p_win, v_rows) + jnp.einsum('bhrqj,bjhd->brqhd', p_ctx, v_ctx)
    return out.reshape(b, n, h * dh)


def context_attention(q, k, v):
    b, l, h, dh = q.shape
    s = jnp.einsum('bihd,bjhd->bhij', q, k, preferred_element_type=jnp.float32) * (HEAD_DIM ** -0.5)
    p = jax.nn.softmax(s, axis=-1).astype(v.dtype)
    return jnp.einsum('bhij,bjhd->bihd', p, v).reshape(b, l, h * dh)


def short_gated_conv(bg, cg, u, w):
    z = cg * u
    zc = lax.conv_general_dilated(z, w[:, None, :], window_strides=(1,),
                                  padding=[(CONV_K // 2, CONV_K // 2)],
                                  dimension_numbers=('NWC', 'WIO', 'NWC'),
                                  feature_group_count=z.shape[-1])
    return bg * zc


def fourier_mix(f):
    b, n, _ = f.shape
    fg = f.reshape(b, n, FOURIER_GROUPS, FOURIER_GROUP_DIM).astype(jnp.float32)
    y = jnp.fft.fft2(fg, axes=(1, 3), norm='ortho').real
    return y.reshape(b, n, FOURIER_WIDTH).astype(f.dtype)


def merge_groups(na, cv, fo, g_mix_out, w_out):
    y = jnp.concatenate([group_rms(na), group_rms(cv), group_rms(fo)], axis=-1) * g_mix_out.astype(jnp.float32)
    return y.astype(na.dtype) @ w_out


def expert_choice_ffn(h, w_router, w_gate, w_up, w_down):
    b, n, d = h.shape
    cap = CAPACITY_FACTOR * n // N_EXPERTS
    aff = jax.nn.softmax((h @ w_router).astype(jnp.float32), axis=-1)
    gates, idx = lax.top_k(jnp.swapaxes(aff, 1, 2), cap)
    xs = jax.vmap(lambda hb, ib: hb[ib])(h, idx)
    hid = jax.nn.silu(jnp.einsum('becd,edf->becf', xs, w_gate)) * jnp.einsum('becd,edf->becf', xs, w_up)
    out = jnp.einsum('becf,efd->becd', hid, w_down) * gates[..., None].astype(h.dtype)
    return jax.vmap(lambda ib, ob: jnp.zeros((n, d), h.dtype).at[ib.reshape(-1)].add(ob.reshape(-1, d)))(idx, out)


def hybrid_layer(x, ctx, mod_x, mod_c, g_pre_mix, g_post_mix, g_pre_ffn, g_post_ffn,
                 w_in, rpb, conv_w, g_mix_out, w_out, w_router, w_gate, w_up, w_down, ctx_out):
    sh1, sc1, g1, sh2, sc2, g2 = [t[:, None, :] for t in jnp.split(mod_x, N_MOD, axis=-1)]
    sh1c, sc1c, g1c, sh2c, sc2c, g2c = [t[None, None, :] for t in jnp.split(mod_c, N_MOD, axis=-1)]
    h = modulate(rms_norm(x, g_pre_mix), sh1, sc1)
    hc = modulate(rms_norm(ctx, g_pre_mix), sh1c, sc1c)
    q, k, v, bg, cg, u, f = split_proj(h @ w_in)
    qc, kc, vc, bgc, cgc, uc, fc = split_proj(hc @ w_in)
    na = neighbourhood_attention(q, k, v, kc, vc, rpb)
    cv = short_gated_conv(bg, cg, u, conv_w)
    fo = fourier_mix(f)
    x = x + g1 * rms_norm(merge_groups(na, cv, fo, g_mix_out, w_out), g_post_mix)
    h2 = modulate(rms_norm(x, g_pre_ffn), sh2, sc2)
    x = x + g2 * rms_norm(expert_choice_ffn(h2, w_router, w_gate, w_up, w_down), g_post_ffn)
    if ctx_out:
        na_c = context_attention(qc, kc, vc)
        cv_c = short_gated_conv(bgc, cgc, uc, conv_w)
        fo_c = fourier_mix(fc)
        ctx = ctx + g1c * rms_norm(merge_groups(na_c, cv_c, fo_c, g_mix_out, w_out), g_post_mix)
        h2c = modulate(rms_norm(ctx, g_pre_ffn), sh2c, sc2c)
        ctx = ctx + g2c * rms_norm(expert_choice_ffn(h2c, w_router, w_gate, w_up, w_down), g_post_ffn)
    return x, ctx


def setup_inputs(seed: int = 0) -> dict:
    key = jax.random.key(seed)
    ks = jax.random.split(key, 20)
    d = D_MODEL
    nrm = lambda k, shape, s: jax.random.normal(k, shape, jnp.float32) * s
    return {
        'x': nrm(ks[0], (BATCH, SEQ, d), 1.0),
        'c': nrm(ks[1], (BATCH, d), 1.0),
        'ctx': nrm(ks[2], (BATCH, CTX_LEN, d), 1.0),
        'c_ctx': nrm(ks[3], (d,), 1.0),
        'w_mod': nrm(ks[4], (DEPTH, d, N_MOD * d), 0.5 * d ** -0.5),
        'b_mod': nrm(ks[5], (DEPTH, N_MOD * d), 0.02),
        'g_pre_mix': 1.0 + nrm(ks[6], (DEPTH, d), 0.05),
        'g_post_mix': 1.0 + nrm(ks[7], (DEPTH, d), 0.05),
        'g_pre_ffn': 1.0 + nrm(ks[8], (DEPTH, d), 0.05),
        'g_post_ffn': 1.0 + nrm(ks[9], (DEPTH, d), 0.05),
        'w_in': nrm(ks[10], (DEPTH, d, IN_WIDTH), d ** -0.5),
        'rpb': nrm(ks[11], (DEPTH, NA_HEADS, 2 * NA_ROWS - 1, 2 * NA_COLS - 1), 0.1),
        'conv_w': nrm(ks[12], (DEPTH, CONV_K, CONV_CH), CONV_K ** -0.5),
        'g_mix_out': 1.0 + nrm(ks[13], (DEPTH, d), 0.05),
        'w_out': nrm(ks[14], (DEPTH, d, d), d ** -0.5),
        'w_router': nrm(ks[15], (DEPTH, d, N_EXPERTS), d ** -0.5),
        'w_gate': nrm(ks[16], (DEPTH, N_EXPERTS, d, EXPERT_HIDDEN), d ** -0.5),
        'w_up': nrm(ks[17], (DEPTH, N_EXPERTS, d, EXPERT_HIDDEN), d ** -0.5),
        'w_down': nrm(ks[18], (DEPTH, N_EXPERTS, EXPERT_HIDDEN, d), EXPERT_HIDDEN ** -0.5),
    }


def reference(x, c, ctx, c_ctx, w_mod, b_mod, g_pre_mix, g_post_mix, g_pre_ffn, g_post_ffn,
              w_in, rpb, conv_w, g_mix_out, w_out, w_router, w_gate, w_up, w_down):
    silu_c = jax.nn.silu(c)
    silu_cc = jax.nn.silu(c_ctx)
    for layer in range(DEPTH):
        mod_x = silu_c @ w_mod[layer] + b_mod[layer]
        mod_c = silu_cc @ w_mod[layer] + b_mod[layer]
        x, ctx = hybrid_layer(x, ctx, mod_x, mod_c, g_pre_mix[layer], g_post_mix[layer],
                              g_pre_ffn[layer], g_post_ffn[layer], w_in[layer], rpb[layer],
                              conv_w[layer], g_mix_out[layer], w_out[layer], w_router[layer],
                              w_gate[layer], w_up[layer], w_down[layer], layer < DEPTH - 1)
    return x
```

```python
import functools
import math

import numpy as np
import jax
import jax.numpy as jnp
from jax import lax
from jax.experimental import pallas as pl
from jax.experimental.pallas import tpu as pltpu

f32 = jnp.float32
bf16 = jnp.bfloat16
i32 = jnp.int32

D_MODEL = 1024
GRID_W = 64
HEAD_DIM = 64
NA_WIDTH = 512
NA_HEADS = 8
NA_ROWS = 8
NA_COLS = 16
CONV_CH = 256
FOURIER_WIDTH = 256
FOURIER_GROUP_DIM = 64
N_EXPERTS = 16
CAPACITY_FACTOR = 2
N_MOD = 6
RMS_EPS = 1e-6
NEG_INF = -1e30

LANES = 128
SUBLANES = 8
ROW_TILES = D_MODEL // LANES
VMEM_LIMIT = 56 * 1024 * 1024

_NT = (((1,), (1,)), ((), ()))


def _cparams(sem, vmem=None):
    return pltpu.CompilerParams(dimension_semantics=sem, vmem_limit_bytes=vmem)


def _rms(x):
    return x * lax.rsqrt(jnp.mean(x * x, axis=-1, keepdims=True) + RMS_EPS)


def _ones_where(mask, dtype=bf16):
    return jnp.where(mask, 1.0, 0.0).astype(dtype)


def _mod_kernel(c_ref, w_ref, b_ref, o_ref):
    c = c_ref[...]
    s = c * jax.nn.sigmoid(c)
    o_ref[...] = jnp.dot(s, w_ref[...], preferred_element_type=f32,
                         precision=lax.Precision.HIGHEST) + b_ref[...]


def _modulation(cvec, w_mod, b_mod):
    depth, d, n6 = w_mod.shape
    tn = 1536
    return pl.pallas_call(
        _mod_kernel,
        grid=(depth, n6 // tn),
        in_specs=[
            pl.BlockSpec((SUBLANES, d), lambda l, j: (0, 0)),
            pl.BlockSpec((None, d, tn), lambda l, j: (l, 0, j)),
            pl.BlockSpec((None, 1, tn), lambda l, j: (l, 0, j)),
        ],
        out_specs=pl.BlockSpec((None, SUBLANES, tn), lambda l, j: (l, 0, j)),
        out_shape=jax.ShapeDtypeStruct((depth, SUBLANES, n6), f32),
        compiler_params=_cparams(("parallel", "parallel"), 40 * 1024 * 1024),
        name="modulation",
    )(cvec, w_mod, b_mod.reshape(depth, 1, n6))


def _inproj_kernel(x_ref, sh_ref, sc_ref, g_ref, w_ref, q_ref, k_ref, v_ref, bg_ref, z_ref, f_ref):
    x = x_ref[...]
    h = (_rms(x) * g_ref[...]) * (1.0 + sc_ref[...]) + sh_ref[...]
    hb = h.astype(bf16)

    def proj(a, b):
        return jnp.dot(hb, w_ref[:, a:b], preferred_element_type=f32)

    o = 3 * NA_WIDTH
    q_ref[...] = (proj(0, NA_WIDTH) * (HEAD_DIM ** -0.5)).astype(bf16)
    k_ref[...] = proj(NA_WIDTH, 2 * NA_WIDTH).astype(bf16)
    v_ref[...] = proj(2 * NA_WIDTH, o).astype(bf16)
    bg_ref[...] = proj(o, o + CONV_CH)
    z_ref[...] = proj(o + CONV_CH, o + 2 * CONV_CH) * proj(o + 2 * CONV_CH, o + 3 * CONV_CH)
    f_ref[...] = proj(o + 3 * CONV_CH, o + 3 * CONV_CH + FOURIER_WIDTH).astype(bf16)


def _in_projection(x, modl, mod_row, g_pre, w_in_b, tm):
    b, n, d = x.shape
    in_w = w_in_b.shape[1]
    row = lambda c: pl.BlockSpec((None, None, 1, d), lambda bi, i: (mod_row(bi), c, 0, 0))
    tok = lambda w: pl.BlockSpec((None, tm, w), lambda bi, i: (bi, i, 0))
    shp = lambda w, dt: jax.ShapeDtypeStruct((b, n, w), dt)
    return pl.pallas_call(
        _inproj_kernel,
        grid=(b, n // tm),
        in_specs=[tok(d), row(0), row(1),
                  pl.BlockSpec((1, d), lambda bi, i: (0, 0)),
                  pl.BlockSpec((d, in_w), lambda bi, i: (0, 0))],
        out_specs=[tok(NA_WIDTH), tok(NA_WIDTH), tok(NA_WIDTH), tok(CONV_CH), tok(CONV_CH), tok(FOURIER_WIDTH)],
        out_shape=[shp(NA_WIDTH, bf16), shp(NA_WIDTH, bf16), shp(NA_WIDTH, bf16),
                   shp(CONV_CH, f32), shp(CONV_CH, f32), shp(FOURIER_WIDTH, bf16)],
        compiler_params=_cparams(("parallel", "parallel"), 40 * 1024 * 1024),
        name="in_projection",
    )(x, modl, modl, g_pre.reshape(1, d), w_in_b)


def _softmax_pv(parts):
    m = None
    for s, _ in parts:
        mi = jnp.max(s, axis=-1, keepdims=True)
        m = mi if m is None else jnp.maximum(m, mi)
    l = None
    o = None
    for s, v in parts:
        e = jnp.exp(s - m)
        li = jnp.sum(e, axis=-1, keepdims=True)
        oi = jnp.dot(e.astype(bf16), v, preferred_element_type=f32)
        l = li if l is None else l + li
        o = oi if o is None else o + oi
    return o * (1.0 / l)


def _na_kernel(q_ref, k_ref, v_ref, kc_ref, vc_ref, bias_ref, o_ref, *, rows, rb):
    i = pl.program_id(1)
    win = NA_ROWS * GRID_W
    lo = lax.broadcasted_iota(i32, (GRID_W, LANES), 1) < HEAD_DIM

    def row_body(j, carry):
        r = i * rb + j
        rs = jnp.clip(r - NA_ROWS // 2, 0, rows - NA_ROWS)
        sh = r - rs
        kstart = pl.multiple_of(rs * GRID_W, GRID_W)
        qstart = pl.multiple_of(j * GRID_W, GRID_W)
        for p in range(NA_HEADS // 2):
            ls = slice(p * LANES, (p + 1) * LANES)
            qp = q_ref[pl.ds(qstart, GRID_W), ls]
            kw = k_ref[pl.ds(kstart, win), ls]
            vw = v_ref[pl.ds(kstart, win), ls]
            kcp = kc_ref[:, ls]
            vcp = vc_ref[:, ls]
            outs = []
            for par in range(2):
                qm = jnp.where(lo if par == 0 else jnp.logical_not(lo), qp, jnp.zeros_like(qp))
                s_w = lax.dot_general(qm, kw, _NT, preferred_element_type=f32) + bias_ref[2 * p + par, sh]
                s_c = lax.dot_general(qm, kcp, _NT, preferred_element_type=f32)
                outs.append(_softmax_pv([(s_w, vw), (s_c, vcp)]))
            o_ref[pl.ds(qstart, GRID_W), ls] = jnp.where(lo, outs[0], outs[1])
        return carry

    lax.fori_loop(0, rb, row_body, 0)


def _na_bias_table(rpb_l):
    ik = jnp.arange(NA_ROWS)
    shv = jnp.arange(NA_ROWS)
    dr = ik[None, :] - shv[:, None] + (NA_ROWS - 1)
    cols = jnp.arange(GRID_W)
    dc = jnp.clip(cols[None, :] - cols[:, None], -(NA_COLS - 1), NA_COLS - 1) + (NA_COLS - 1)
    col_start = jnp.clip(cols - NA_COLS // 2, 0, GRID_W - NA_COLS)
    col_in = (cols[None, :] >= col_start[:, None]) & (cols[None, :] < col_start[:, None] + NA_COLS)
    t = rpb_l[:, dr[:, None, :, None], dc[None, :, None, :]]
    t = jnp.where(col_in[None, None, :, None, :], t, NEG_INF)
    return t.reshape(NA_HEADS, NA_ROWS, GRID_W, NA_ROWS * GRID_W).astype(f32)


def _neighbourhood_attention(q, k, v, kc, vc, bias):
    b, n, w = q.shape
    l = kc.shape[1]
    rows = n // GRID_W
    rb = 8
    full = lambda m: pl.BlockSpec((None, m, w), lambda bi, i: (bi, 0, 0))
    return pl.pallas_call(
        functools.partial(_na_kernel, rows=rows, rb=rb),
        grid=(b, rows // rb),
        in_specs=[pl.BlockSpec((None, rb * GRID_W, w), lambda bi, i: (bi, i, 0)),
                  full(n), full(n), full(l), full(l),
                  pl.BlockSpec(bias.shape, lambda bi, i: (0, 0, 0, 0))],
        out_specs=pl.BlockSpec((None, rb * GRID_W, w), lambda bi, i: (bi, i, 0)),
        out_shape=jax.ShapeDtypeStruct((b, n, w), f32),
        compiler_params=_cparams(("parallel", "arbitrary"), 48 * 1024 * 1024),
        name="neighbourhood_attention",
    )(q, k, v, kc, vc, bias)


def _ctx_attn_kernel(q_ref, k_ref, v_ref, o_ref):
    m = q_ref.shape[0]
    lo = lax.broadcasted_iota(i32, (m, LANES), 1) < HEAD_DIM
    for p in range(NA_HEADS // 2):
        ls = slice(p * LANES, (p + 1) * LANES)
        qp, kp, vp = q_ref[:, ls], k_ref[:, ls], v_ref[:, ls]
        outs = []
        for par in range(2):
            qm = jnp.where(lo if par == 0 else jnp.logical_not(lo), qp, jnp.zeros_like(qp))
            s = lax.dot_general(qm, kp, _NT, preferred_element_type=f32)
            outs.append(_softmax_pv([(s, vp)]))
        o_ref[:, ls] = jnp.where(lo, outs[0], outs[1])


def _context_attention(q, k, v):
    b, l, w = q.shape
    spec = pl.BlockSpec((None, l, w), lambda bi: (bi, 0, 0))
    return pl.pallas_call(
        _ctx_attn_kernel, grid=(b,), in_specs=[spec, spec, spec], out_specs=spec,
        out_shape=jax.ShapeDtypeStruct((b, l, w), f32),
        compiler_params=_cparams(("parallel",)), name="context_attention",
    )(q, k, v)


def _dft_tables_channel():
    c = np.arange(FOURIER_GROUP_DIM)
    ang = 2.0 * np.pi * np.outer(c, c) / FOURIER_GROUP_DIM
    g = FOURIER_WIDTH // FOURIER_GROUP_DIM
    cb = np.kron(np.eye(g), np.cos(ang))
    sb = np.kron(np.eye(g), np.sin(ang))
    return np.concatenate([cb, -sb], axis=1)


def _fourier_kernel(f_ref, cs_ref, d1_ref, d3_ref, tc_ref, ts_ref, o_ref, g_scr, br_scr, bi_scr, o_scr, *, r):
    nt = FOURIER_WIDTH // LANES
    g = jnp.dot(f_ref[...], cs_ref[...], preferred_element_type=f32)
    for c in range(2 * nt):
        g_scr[c] = g[:, c * LANES:(c + 1) * LANES]
    d1 = d1_ref[...]

    def strided(scr, tiles, start, size, stride):
        return jnp.concatenate([scr[c, pl.ds(start, size, stride=stride), :] for c in tiles], axis=1)

    for n2 in range(GRID_W):
        gr = strided(g_scr, range(nt), n2, r, GRID_W)
        gi = strided(g_scr, range(nt, 2 * nt), n2, r, GRID_W)
        rhs = jnp.concatenate([gr, gi], axis=0).astype(bf16)
        a = jnp.dot(d1, rhs, preferred_element_type=f32)
        ar, ai = a[:r], a[r:]
        tc = tc_ref[n2 * r:(n2 + 1) * r, :]
        ts = ts_ref[n2 * r:(n2 + 1) * r, :]
        br = ar * tc + ai * ts
        bi = ai * tc - ar * ts
        for c in range(nt):
            br_scr[c, n2 * r:(n2 + 1) * r, :] = br[:, c * LANES:(c + 1) * LANES]
            bi_scr[c, n2 * r:(n2 + 1) * r, :] = bi[:, c * LANES:(c + 1) * LANES]
    d3 = d3_ref[...]
    for k1 in range(r):
        br = strided(br_scr, range(nt), k1, GRID_W, r)
        bi = strided(bi_scr, range(nt), k1, GRID_W, r)
        rhs = jnp.concatenate([br, bi], axis=0).astype(bf16)
        y = jnp.dot(d3, rhs, preferred_element_type=f32)
        for c in range(nt):
            o_scr[c, pl.ds(k1, GRID_W, stride=r), :] = y[:, c * LANES:(c + 1) * LANES]
    for c in range(nt):
        o_ref[:, c * LANES:(c + 1) * LANES] = o_scr[c]


def _fourier_mix(f):
    b, n, w = f.shape
    r = n // GRID_W
    cs = jnp.asarray(_dft_tables_channel(), bf16)
    a1 = 2.0 * np.pi * np.outer(np.arange(r), np.arange(r)) / r
    c1, s1 = np.cos(a1), np.sin(a1)
    d1 = jnp.asarray(np.block([[c1, s1], [-s1, c1]]), bf16)
    a3 = 2.0 * np.pi * np.outer(np.arange(GRID_W), np.arange(GRID_W)) / GRID_W
    d3 = jnp.asarray(np.concatenate([np.cos(a3), np.sin(a3)], axis=1), bf16)
    n2 = np.arange(GRID_W)[:, None]
    k1 = np.arange(r)[None, :]
    at = 2.0 * np.pi * ((n2 * k1) % n) / n
    scale = 1.0 / math.sqrt(n * FOURIER_GROUP_DIM)
    tc = jnp.asarray(np.broadcast_to((np.cos(at) * scale).reshape(n, 1), (n, w)), f32)
    ts = jnp.asarray(np.broadcast_to((np.sin(at) * scale).reshape(n, 1), (n, w)), f32)
    const = lambda a: pl.BlockSpec(a.shape, lambda bi: (0,) * a.ndim)
    return pl.pallas_call(
        functools.partial(_fourier_kernel, r=r),
        grid=(b,),
        in_specs=[pl.BlockSpec((None, n, w), lambda bi: (bi, 0, 0)),
                  const(cs), const(d1), const(d3), const(tc), const(ts)],
        out_specs=pl.BlockSpec((None, n, w), lambda bi: (bi, 0, 0)),
        out_shape=jax.ShapeDtypeStruct((b, n, w), f32),
        scratch_shapes=[pltpu.VMEM((2 * w // LANES, n, LANES), f32)] + [pltpu.VMEM((w // LANES, n, LANES), f32)] * 3,
        compiler_params=_cparams(("parallel",), 48 * 1024 * 1024),
        name="fourier_mix",
    )(f, cs, d1, d3, tc, ts)


def _fourier_direct_kernel(f_ref, cs_ref, dn_ref, o_ref):
    w = FOURIER_WIDTH
    g = jnp.dot(f_ref[...], cs_ref[...], preferred_element_type=f32)
    rhs = jnp.concatenate([g[:, :w], g[:, w:]], axis=0).astype(bf16)
    o_ref[...] = jnp.dot(dn_ref[...], rhs, preferred_element_type=f32)


def _fourier_mix_direct(f):
    b, n, w = f.shape
    cs = jnp.asarray(_dft_tables_channel(), bf16)
    a = 2.0 * np.pi * (np.outer(np.arange(n), np.arange(n)) % n) / n
    scale = 1.0 / math.sqrt(n * FOURIER_GROUP_DIM)
    dn = jnp.asarray(np.concatenate([np.cos(a), np.sin(a)], axis=1) * scale, bf16)
    const = lambda t: pl.BlockSpec(t.shape, lambda bi: (0,) * t.ndim)
    return pl.pallas_call(
        _fourier_direct_kernel, grid=(b,),
        in_specs=[pl.BlockSpec((None, n, w), lambda bi: (bi, 0, 0)), const(cs), const(dn)],
        out_specs=pl.BlockSpec((None, n, w), lambda bi: (bi, 0, 0)),
        out_shape=jax.ShapeDtypeStruct((b, n, w), f32),
        compiler_params=_cparams(("parallel",)), name="fourier_mix_direct",
    )(f, cs, dn)


def _to_rows(t):
    st = jnp.stack([t[:, c * LANES:(c + 1) * LANES] for c in range(ROW_TILES)], axis=0)
    return pltpu.einshape("hmd->mhd", st)


def _from_rows(rows):
    t = pltpu.einshape("mhd->hmd", rows)
    return [t[c] for c in range(ROW_TILES)]


def _merge_kernel(na_ref, bg_ref, z_ref, zp_ref, zn_ref, fo_ref, x_ref, cw_ref, gmix_ref, wout_ref,
                  gpost_ref, g1_ref, sh2_ref, sc2_ref, gpre_ref, wr_ref,
                  xo_ref, h2_ref, lg_ref):
    i = pl.program_id(1)
    nt = pl.num_programs(1)
    tm = z_ref.shape[0]
    z = z_ref[...]
    zp = jnp.where(i > 0, zp_ref[SUBLANES - 1:SUBLANES, :], 0.0)
    zn = jnp.where(i < nt - 1, zn_ref[0:1, :], 0.0)
    rid = lax.broadcasted_iota(i32, z.shape, 0)
    z_dn = jnp.where(rid == 0, zp, pltpu.roll(z, 1, axis=0))
    z_up = jnp.where(rid == tm - 1, zn, pltpu.roll(z, tm - 1, axis=0))
    cv = bg_ref[...] * (cw_ref[0:1, :] * z_dn + cw_ref[1:2, :] * z + cw_ref[2:3, :] * z_up)

    a, b_ = NA_WIDTH, NA_WIDTH + CONV_CH
    parts = ((na_ref[...], 0, a), (cv, a, b_), (fo_ref[...], b_, D_MODEL))
    m = None
    for t, lo, hi in parts:
        y = (_rms(t) * gmix_ref[:, lo:hi]).astype(bf16)
        d = jnp.dot(y, wout_ref[lo:hi, :], preferred_element_type=f32)
        m = d if m is None else m + d
    xn = x_ref[...] + g1_ref[...] * (_rms(m) * gpost_ref[...])
    xo_ref[...] = xn
    h2 = (_rms(xn) * gpre_ref[...]) * (1.0 + sc2_ref[...]) + sh2_ref[...]
    h2_ref[...] = _to_rows(h2)
    hh = h2.astype(bf16)
    hl = (h2 - hh.astype(f32)).astype(bf16)
    wr = wr_ref[...]
    wh = wr.astype(bf16)
    wl = (wr - wh.astype(f32)).astype(bf16)
    nt_dot = lambda u, v: lax.dot_general(u, v, _NT, preferred_element_type=f32)
    lg_ref[...] = nt_dot(wh, hh) + (nt_dot(wh, hl) + nt_dot(wl, hh))


def _merge(na, bg, z, fo, x, modl, mod_row, conv_w, g_mix, w_out_b, g_post, g_pre_ffn, w_router_t, tm):
    b, n, d = x.shape
    hb = tm // SUBLANES
    nhb = n // SUBLANES
    row = lambda c: pl.BlockSpec((None, None, 1, d), lambda bi, i: (mod_row(bi), c, 0, 0))
    tok = lambda w: pl.BlockSpec((None, tm, w), lambda bi, i: (bi, i, 0))
    const = lambda a: pl.BlockSpec(a.shape, lambda bi, i: (0,) * a.ndim)
    vec = lambda a: a.reshape(1, -1)
    nt = n // tm
    args = (na, bg, z, z, z, fo, x, conv_w, vec(g_mix), w_out_b, vec(g_post), modl, modl, modl,
            vec(g_pre_ffn), w_router_t)
    return pl.pallas_call(
        _merge_kernel,
        grid=(b, nt),
        in_specs=[tok(NA_WIDTH), tok(CONV_CH), tok(CONV_CH),
                  pl.BlockSpec((None, SUBLANES, CONV_CH), lambda bi, i: (bi, jnp.maximum(i * hb - 1, 0), 0)),
                  pl.BlockSpec((None, SUBLANES, CONV_CH), lambda bi, i: (bi, jnp.minimum((i + 1) * hb, nhb - 1), 0)),
                  tok(FOURIER_WIDTH), tok(d), const(conv_w), const(vec(g_mix)), const(w_out_b),
                  const(vec(g_post)), row(2), row(3), row(4), const(vec(g_pre_ffn)), const(w_router_t)],
        out_specs=[tok(d),
                   pl.BlockSpec((tm, ROW_TILES, LANES), lambda bi, i: (bi * nt + i, 0, 0)),
                   pl.BlockSpec((None, N_EXPERTS, tm), lambda bi, i: (bi, 0, i))],
        out_shape=[jax.ShapeDtypeStruct((b, n, d), f32),
                   jax.ShapeDtypeStruct((b * n, ROW_TILES, LANES), f32),
                   jax.ShapeDtypeStruct((b, N_EXPERTS, n), f32)],
        compiler_params=_cparams(("parallel", "parallel"), 40 * 1024 * 1024),
        name="merge",
    )(*args)


def _split3(a):
    h = a.astype(bf16)
    r1 = a - h.astype(f32)
    m = r1.astype(bf16)
    l = (r1 - m.astype(f32)).astype(bf16)
    return h, m, l


def _threshold_search(bits, count, cap):
    t = jnp.zeros(bits.shape, i32)
    for bit in range(30, -1, -1):
        cand = t | (1 << bit)
        t = jnp.where(count(bits >= cand) >= cap, cand, t)
    return t


def _route_kernel(lg_ref, o_ref, *, nch, cap):
    ne = N_EXPERTS
    r = ne * nch
    sh = nch.bit_length() - 1
    lg = lg_ref[...]
    ex = jnp.exp(lg - jnp.max(lg, axis=0, keepdims=True))
    aff = (ex / jnp.sum(ex, axis=0, keepdims=True)).reshape(r, LANES)
    bits = lax.bitcast_convert_type(aff, i32)

    li = lax.broadcasted_iota(i32, (LANES, LANES), 0)
    lj = lax.broadcasted_iota(i32, (LANES, LANES), 1)
    triu = _ones_where(li <= lj)
    tril = _ones_where(lj <= li)
    eye = _ones_where(li == lj)
    ones = jnp.ones((LANES, LANES), bf16)
    ri = lax.broadcasted_iota(i32, (r, r), 0)
    rj = lax.broadcasted_iota(i32, (r, r), 1)
    same = (ri >> sh) == (rj >> sh)
    bd = _ones_where(same)
    sl = _ones_where(same & (rj < ri))
    mm = lambda u, v: jnp.dot(u, v, preferred_element_type=f32)

    def count(mask):
        return mm(bd, mm(_ones_where(mask), ones).astype(bf16))

    thr = _threshold_search(bits, count, cap)
    gt = bits > thr
    eq = bits == thr
    need = cap - count(gt)
    eqb = _ones_where(eq)
    pe = mm(eqb, triu) + mm(sl, mm(eqb, ones).astype(bf16))
    sel = gt | (eq & (pe <= need))
    selb = _ones_where(sel)
    rt = mm(selb, ones)
    oin = mm(sl, rt.astype(bf16)) + rt

    reps = cap // LANES
    tile = lambda a: jnp.concatenate([a] * reps, axis=1) if reps > 1 else a
    s_f = lax.broadcasted_iota(i32, (nch, cap), 1).astype(f32)
    s8 = lax.broadcasted_iota(i32, (SUBLANES, cap), 1).astype(f32)
    ji = lax.broadcasted_iota(i32, (nch, cap), 0).astype(f32)
    lis = lax.broadcasted_iota(i32, (LANES, cap), 0).astype(f32)
    ri8 = lax.broadcasted_iota(i32, (SUBLANES, cap), 0)
    ones8c = jnp.ones((SUBLANES, nch), bf16)
    ones8l = jnp.ones((SUBLANES, LANES), bf16)
    for e in range(ne):
        rows = slice(e * nch, (e + 1) * nch)
        zmask = tile(oin[rows]) <= s_f
        js = mm(ones8c, _ones_where(zmask))
        oex = mm(ones8c, jnp.where(zmask, tile(rt[rows]), 0.0).astype(bf16))
        s_loc = s8 - oex
        u = _ones_where(ji == jnp.broadcast_to(js[0:1], (nch, cap)))
        plt = lax.dot_general(tril, selb[rows], _NT, preferred_element_type=f32)
        ptg = mm(plt.astype(bf16), u)
        c = _ones_where(ptg <= jnp.broadcast_to(s_loc[0:1], (LANES, cap)))
        tl = mm(ones8l, c)
        idx = js * float(LANES) + tl
        affg = None
        for piece in _split3(aff[rows]):
            pt = lax.dot_general(eye, piece, _NT, preferred_element_type=f32).astype(bf16)
            pg = mm(pt, u)
            affg = pg if affg is None else affg + pg
        hit = lis == jnp.broadcast_to(tl[0:1], (LANES, cap))
        gate = jnp.sum(jnp.where(hit, affg, 0.0), axis=0, keepdims=True)
        o_ref[e] = jnp.where(ri8 == 0, idx, jnp.where(ri8 == 1, jnp.broadcast_to(gate, (SUBLANES, cap)), 0.0))


def _route_small_kernel(lg_ref, o_ref, *, cap):
    ne, n = lg_ref.shape
    lg = lg_ref[...]
    ex = jnp.exp(lg - jnp.max(lg, axis=0, keepdims=True))
    aff = ex / jnp.sum(ex, axis=0, keepdims=True)
    bits = lax.bitcast_convert_type(aff, i32)
    li = lax.broadcasted_iota(i32, (n, n), 0)
    lj = lax.broadcasted_iota(i32, (n, n), 1)
    triu = _ones_where(li <= lj)
    ones = jnp.ones((n, n), bf16)
    mm = lambda u, v: jnp.dot(u, v, preferred_element_type=f32)
    count = lambda mask: mm(_ones_where(mask), ones)
    thr = _threshold_search(bits, count, cap)
    gt = bits > thr
    eq = bits == thr
    need = cap - count(gt)
    sel = gt | (eq & (mm(_ones_where(eq), triu) <= need))
    slot = jnp.where(sel, mm(_ones_where(sel), triu) - 1.0, -1.0)
    tid = lax.broadcasted_iota(i32, (SUBLANES, n), 1).astype(f32)
    ri = lax.broadcasted_iota(i32, (SUBLANES, n), 0)
    s_i = lax.broadcasted_iota(i32, (cap, n), 0).astype(f32)
    ro = lax.broadcasted_iota(i32, (SUBLANES, cap), 0)
    bc = lambda a: jnp.broadcast_to(a, (SUBLANES, n))
    for e in range(ne):
        et = _ones_where(s_i == jnp.broadcast_to(slot[e:e + 1], (cap, n)))
        h, m, l = _split3(aff[e:e + 1])
        rm = jnp.where(ri == 0, tid, jnp.where(ri == 1, bc(h.astype(f32)), jnp.where(
            ri == 2, bc(m.astype(f32)), jnp.where(ri == 3, bc(l.astype(f32)), 0.0)))).astype(bf16)
        res = lax.dot_general(rm, et, _NT, preferred_element_type=f32)
        gate = res[1:2] + res[2:3] + res[3:4]
        o_ref[e] = jnp.where(ro == 0, jnp.broadcast_to(res[0:1], (SUBLANES, cap)),
                             jnp.where(ro == 1, jnp.broadcast_to(gate, (SUBLANES, cap)), 0.0))


def _route(logits_t):
    b, ne, n = logits_t.shape
    cap = CAPACITY_FACTOR * n // N_EXPERTS
    out_shape = jax.ShapeDtypeStruct((b, ne, SUBLANES, cap), f32)
    out_spec = pl.BlockSpec((None, ne, SUBLANES, cap), lambda bi: (bi, 0, 0, 0))
    nch = n // LANES
    if nch % SUBLANES == 0 and cap % LANES == 0:
        res = pl.pallas_call(
            functools.partial(_route_kernel, nch=nch, cap=cap), grid=(b,),
            in_specs=[pl.BlockSpec((None, ne, nch, LANES), lambda bi: (bi, 0, 0, 0))],
            out_specs=out_spec, out_shape=out_shape,
            compiler_params=_cparams(("parallel",), 40 * 1024 * 1024), name="route",
        )(logits_t.reshape(b, ne, nch, LANES))
    else:
        res = pl.pallas_call(
            functools.partial(_route_small_kernel, cap=cap), grid=(b,),
            in_specs=[pl.BlockSpec((None, ne, n), lambda bi: (bi, 0, 0))],
            out_specs=out_spec, out_shape=out_shape,
            compiler_params=_cparams(("parallel",)), name="route_small",
        )(logits_t)
    idx = res[:, :, 0, :].astype(i32).reshape(b * ne, 1, cap)
    gates = res[:, :, 1, :].reshape(b * ne, 1, cap)
    return idx, gates


def _dispatch_kernel(idx_ref, h_ref, o_ref, scr):
    cap = o_ref.shape[0]

    def body(i, c):
        scr[i] = h_ref[idx_ref[0, 0, i]]
        return c

    lax.fori_loop(0, cap, body, 0, unroll=8)
    tiles = _from_rows(scr[...])
    for c in range(ROW_TILES):
        o_ref[:, c * LANES:(c + 1) * LANES] = tiles[c].astype(bf16)


def _dispatch(h2_rows, idx, b):
    n = h2_rows.shape[0] // b
    cap = idx.shape[2]
    ne = N_EXPERTS
    return pl.pallas_call(
        _dispatch_kernel,
        grid=(b, ne),
        in_specs=[pl.BlockSpec((1, 1, cap), lambda bi, e: (bi * ne + e, 0, 0), memory_space=pltpu.SMEM),
                  pl.BlockSpec((n, ROW_TILES, LANES), lambda bi, e: (bi, 0, 0))],
        out_specs=pl.BlockSpec((None, cap, D_MODEL), lambda bi, e: (e, bi, 0)),
        out_shape=jax.ShapeDtypeStruct((ne, b * cap, D_MODEL), bf16),
        scratch_shapes=[pltpu.VMEM((cap, ROW_TILES, LANES), f32)],
        compiler_params=_cparams(("parallel", "arbitrary"), 48 * 1024 * 1024),
        name="dispatch",
    )(idx, h2_rows)


def _combine_kernel(idx_ref, g_ref, y_ref, o_ref, acc, ys, *, chunk):
    s = pl.program_id(1)
    ne = N_EXPERTS
    cap = y_ref.shape[0]

    @pl.when(s == 0)
    def _():
        acc[...] = jnp.zeros_like(acc)

    @pl.when(s < ne)
    def _():
        ys[...] = _to_rows(y_ref[...].astype(f32))

        def body(i, c):
            t = idx_ref[0, 0, i]
            acc[t] = acc[t] + g_ref[0, 0, i] * ys[i]
            return c

        lax.fori_loop(0, cap, body, 0, unroll=8)

    @pl.when(s >= ne)
    def _():
        start = pl.multiple_of((s - ne) * chunk, chunk)
        tiles = _from_rows(acc[pl.ds(start, chunk)])
        for c in range(ROW_TILES):
            o_ref[:, c * LANES:(c + 1) * LANES] = tiles[c]


def _combine(y, idx, gates, b, n):
    ne = N_EXPERTS
    cap = idx.shape[2]
    chunk = min(n, 512)
    nchunk = n // chunk
    last = ne - 1
    sm = lambda: pl.BlockSpec((1, 1, cap), lambda bi, s: (bi * ne + jnp.minimum(s, last), 0, 0),
                              memory_space=pltpu.SMEM)
    return pl.pallas_call(
        functools.partial(_combine_kernel, chunk=chunk),
        grid=(b, ne + nchunk),
        in_specs=[sm(), sm(),
                  pl.BlockSpec((None, cap, D_MODEL), lambda bi, s: (jnp.minimum(s, last), bi, 0))],
        out_specs=pl.BlockSpec((None, chunk, D_MODEL), lambda bi, s: (bi, jnp.maximum(s - ne, 0), 0)),
        out_shape=jax.ShapeDtypeStruct((b, n, D_MODEL), f32),
        scratch_shapes=[pltpu.VMEM((n, ROW_TILES, LANES), f32), pltpu.VMEM((cap, ROW_TILES, LANES), f32)],
        compiler_params=_cparams(("parallel", "arbitrary"), 48 * 1024 * 1024),
        name="combine",
    )(idx, gates, y)


def _ffn_kernel(*refs, nseg, nf, tf):
    xs = refs[:nseg]
    wg_ref, wu_ref, wd_ref = refs[nseg:nseg + 3]
    ys = refs[nseg + 3:2 * nseg + 3]
    hs = refs[2 * nseg + 3:]
    s = pl.program_id(1)

    @pl.when(s < nf)
    def _():
        wg = wg_ref[...].astype(bf16)
        wu = wu_ref[...].astype(bf16)
        for x_ref, h_scr in zip(xs, hs):
            x = x_ref[...]
            g = jnp.dot(x, wg, preferred_element_type=f32)
            u = jnp.dot(x, wu, preferred_element_type=f32)
            h = ((g * jax.nn.sigmoid(g)) * u).astype(bf16)
            for f in range(nf):
                @pl.when(s == f)
                def _(f=f, h=h, h_scr=h_scr):
                    h_scr[:, f * tf:(f + 1) * tf] = h

    @pl.when(s >= nf)
    def _():
        wd = wd_ref[...].astype(bf16)
        for y_ref, h_scr in zip(ys, hs):
            y_ref[...] = jnp.dot(h_scr[...], wd, preferred_element_type=f32)


def _expert_ffn(xs_list, w_gate, w_up, w_down):
    ne, d, fh = w_gate.shape
    tf = 256
    tn = 256
    nf = fh // tf
    nd = d // tn
    nseg = len(xs_list)
    xspec = lambda m: pl.BlockSpec((None, m, d), lambda e, s: (e, 0, 0))
    wspec = pl.BlockSpec((None, d, tf), lambda e, s: (e, 0, jnp.minimum(s, nf - 1)))
    dspec = pl.BlockSpec((None, fh, tn), lambda e, s: (e, 0, jnp.maximum(s - nf, 0)))
    yspec = lambda m: pl.BlockSpec((None, m, tn), lambda e, s: (e, 0, jnp.maximum(s - nf, 0)))
    ms = [x.shape[1] for x in xs_list]
    return pl.pallas_call(
        functools.partial(_ffn_kernel, nseg=nseg, nf=nf, tf=tf),
        grid=(ne, nf + nd),
        in_specs=[xspec(m) for m in ms] + [wspec, wspec, dspec],
        out_specs=[yspec(m) for m in ms],
        out_shape=[jax.ShapeDtypeStruct((ne, m, d), f32) for m in ms],
        scratch_shapes=[pltpu.VMEM((m, fh), bf16) for m in ms],
        compiler_params=_cparams(("parallel", "arbitrary"), VMEM_LIMIT),
        name="expert_ffn",
    )(*xs_list, w_gate, w_up, w_down)


def _post_kernel(x_ref, y_ref, g2_ref, gp_ref, o_ref):
    o_ref[...] = x_ref[...] + g2_ref[...] * (_rms(y_ref[...]) * gp_ref[...])


def _post(x, y, modl, mod_row, g_post, tm):
    b, n, d = x.shape
    tok = pl.BlockSpec((None, tm, d), lambda bi, i: (bi, i, 0))
    return pl.pallas_call(
        _post_kernel, grid=(b, n // tm),
        in_specs=[tok, tok, pl.BlockSpec((None, None, 1, d), lambda bi, i: (mod_row(bi), 5, 0, 0)),
                  pl.BlockSpec((1, d), lambda bi, i: (0, 0))],
        out_specs=tok, out_shape=jax.ShapeDtypeStruct((b, n, d), f32),
        compiler_params=_cparams(("parallel", "parallel")), name="post_ffn",
    )(x, y, modl, g_post.reshape(1, d))


def kernel(x, c, ctx, c_ctx, w_mod, b_mod, g_pre_mix, g_post_mix, g_pre_ffn, g_post_ffn, w_in, rpb,
           conv_w, g_mix_out, w_out, w_router, w_gate, w_up, w_down):
    depth = w_mod.shape[0]
    b, n, d = x.shape
    l_ctx = ctx.shape[1]
    ctx_row = b
    cvec = jnp.concatenate([c, c_ctx[None, :], jnp.zeros((SUBLANES - b - 1, d), f32)], axis=0)
    mod = _modulation(cvec, w_mod, b_mod)
    lat_row = lambda bi: bi
    cx_row = lambda bi: ctx_row
    tm = min(n, 512)
    for layer in range(depth):
        last = layer == depth - 1
        modl = mod[layer].reshape(SUBLANES, N_MOD, 1, d)
        w_in_b = w_in[layer].astype(bf16)
        w_out_b = w_out[layer].astype(bf16)
        w_router_t = w_router[layer].T
        q, k, v, bg, z, f = _in_projection(x, modl, lat_row, g_pre_mix[layer], w_in_b, tm)
        qc, kc, vc, bgc, zc, fc = _in_projection(ctx, modl, cx_row, g_pre_mix[layer], w_in_b, l_ctx)
        na = _neighbourhood_attention(q, k, v, kc, vc, _na_bias_table(rpb[layer]))
        fo = _fourier_mix(f)
        x, h2_rows, logits = _merge(na, bg, z, fo, x, modl, lat_row, conv_w[layer], g_mix_out[layer],
                                    w_out_b, g_post_mix[layer], g_pre_ffn[layer], w_router_t, tm)
        idx, gates = _route(logits)
        xs_list = [_dispatch(h2_rows, idx, b)]
        if not last:
            na_c = _context_attention(qc, kc, vc)
            fo_c = _fourier_mix_direct(fc)
            ctx, h2c_rows, logits_c = _merge(na_c, bgc, zc, fo_c, ctx, modl, cx_row, conv_w[layer],
                                             g_mix_out[layer], w_out_b, g_post_mix[layer],
                                             g_pre_ffn[layer], w_router_t, l_ctx)
            idx_c, gates_c = _route(logits_c)
            xs_list.append(_dispatch(h2c_rows, idx_c, b))
        ys = _expert_ffn(xs_list, w_gate[layer], w_up[layer], w_down[layer])
        x = _post(x, _combine(ys[0], idx, gates, b, n), modl, lat_row, g_post_ffn[layer], tm)
        if not last:
            ctx = _post(ctx, _combine(ys[1], idx_c, gates_c, b, l_ctx), modl, cx_row, g_post_ffn[layer], l_ctx)
    return x
```

```python
import functools
import math

import numpy as np
import jax
import jax.numpy as jnp
from jax import lax
from jax.experimental import pallas as pl
from jax.experimental.pallas import tpu as pltpu

f32 = jnp.float32
bf16 = jnp.bfloat16
i32 = jnp.int32

D_MODEL = 1024
GRID_W = 64
HEAD_DIM = 64
NA_WIDTH = 512
NA_HEADS = 8
NA_ROWS = 8
NA_COLS = 16
CONV_CH = 256
FOURIER_WIDTH = 256
FOURIER_GROUP_DIM = 64
N_EXPERTS = 16
CAPACITY_FACTOR = 2
N_MOD = 6
RMS_EPS = 1e-6
NEG_INF = -1e30

LANES = 128
SUBLANES = 8
ROW_TILES = D_MODEL // LANES
VMEM_LIMIT = 56 * 1024 * 1024

_NT = (((1,), (1,)), ((), ()))


def _cparams(sem, vmem=None):
    return pltpu.CompilerParams(dimension_semantics=sem, vmem_limit_bytes=vmem)


def _rms(x):
    return x * lax.rsqrt(jnp.mean(x * x, axis=-1, keepdims=True) + RMS_EPS)


def _ones_where(mask, dtype=bf16):
    return jnp.where(mask, 1.0, 0.0).astype(dtype)


def _mod_kernel(c_ref, w_ref, b_ref, o_ref):
    c = c_ref[...]
    s = c * jax.nn.sigmoid(c)
    o_ref[...] = jnp.dot(s, w_ref[...], preferred_element_type=f32,
                         precision=lax.Precision.HIGHEST) + b_ref[...]


def _modulation(cvec, w_mod, b_mod):
    depth, d, n6 = w_mod.shape
    tn = 1536
    return pl.pallas_call(
        _mod_kernel,
        grid=(depth, n6 // tn),
        in_specs=[
            pl.BlockSpec((SUBLANES, d), lambda l, j: (0, 0)),
            pl.BlockSpec((None, d, tn), lambda l, j: (l, 0, j)),
            pl.BlockSpec((None, 1, tn), lambda l, j: (l, 0, j)),
        ],
        out_specs=pl.BlockSpec((None, SUBLANES, tn), lambda l, j: (l, 0, j)),
        out_shape=jax.ShapeDtypeStruct((depth, SUBLANES, n6), f32),
        compiler_params=_cparams(("parallel", "parallel"), 40 * 1024 * 1024),
        name="modulation",
    )(cvec, w_mod, b_mod.reshape(depth, 1, n6))


def _inproj_kernel(x_ref, sh_ref, sc_ref, g_ref, w_ref, q_ref, k_ref, v_ref, bg_ref, z_ref, f_ref):
    x = x_ref[...]
    h = (_rms(x) * g_ref[...]) * (1.0 + sc_ref[...]) + sh_ref[...]
    hb = h.astype(bf16)

    def proj(a, b):
        return jnp.dot(hb, w_ref[:, a:b], preferred_element_type=f32)

    o = 3 * NA_WIDTH
    q_ref[...] = (proj(0, NA_WIDTH) * (HEAD_DIM ** -0.5)).astype(bf16)
    k_ref[...] = proj(NA_WIDTH, 2 * NA_WIDTH).astype(bf16)
    v_ref[...] = proj(2 * NA_WIDTH, o).astype(bf16)
    bg_ref[...] = proj(o, o + CONV_CH)
    z_ref[...] = proj(o + CONV_CH, o + 2 * CONV_CH) * proj(o + 2 * CONV_CH, o + 3 * CONV_CH)
    f_ref[...] = proj(o + 3 * CONV_CH, o + 3 * CONV_CH + FOURIER_WIDTH).astype(bf16)


def _in_projection(x, modl, mod_row, g_pre, w_in_b, tm):
    b, n, d = x.shape
    in_w = w_in_b.shape[1]
    row = lambda c: pl.BlockSpec((None, None, 1, d), lambda bi, i: (mod_row(bi), c, 0, 0))
    tok = lambda w: pl.BlockSpec((None, tm, w), lambda bi, i: (bi, i, 0))
    shp = lambda w, dt: jax.ShapeDtypeStruct((b, n, w), dt)
    return pl.pallas_call(
        _inproj_kernel,
        grid=(b, n // tm),
        in_specs=[tok(d), row(0), row(1),
                  pl.BlockSpec((1, d), lambda bi, i: (0, 0)),
                  pl.BlockSpec((d, in_w), lambda bi, i: (0, 0))],
        out_specs=[tok(NA_WIDTH), tok(NA_WIDTH), tok(NA_WIDTH), tok(CONV_CH), tok(CONV_CH), tok(FOURIER_WIDTH)],
        out_shape=[shp(NA_WIDTH, bf16), shp(NA_WIDTH, bf16), shp(NA_WIDTH, bf16),
                   shp(CONV_CH, f32), shp(CONV_CH, f32), shp(FOURIER_WIDTH, bf16)],
        compiler_params=_cparams(("parallel", "parallel"), 40 * 1024 * 1024),
        name="in_projection",
    )(x, modl, modl, g_pre.reshape(1, d), w_in_b)


def _softmax_pv(parts):
    m = None
    for s, _ in parts:
        mi = jnp.max(s, axis=-1, keepdims=True)
        m = mi if m is None else jnp.maximum(m, mi)
    l = None
    o = None
    for s, v in parts:
        e = jnp.exp(s - m)
        li = jnp.sum(e, axis=-1, keepdims=True)
        oi = jnp.dot(e.astype(bf16), v, preferred_element_type=f32)
        l = li if l is None else l + li
        o = oi if o is None else o + oi
    return o * (1.0 / l)


def _na_kernel(q_ref, k_ref, v_ref, kc_ref, vc_ref, bias_ref, o_ref, *, rows, rb):
    i = pl.program_id(1)
    win = NA_ROWS * GRID_W
    lo = lax.broadcasted_iota(i32, (GRID_W, LANES), 1) < HEAD_DIM

    def row_body(j, carry):
        r = i * rb + j
        rs = jnp.clip(r - NA_ROWS // 2, 0, rows - NA_ROWS)
        sh = r - rs
        kstart = pl.multiple_of(rs * GRID_W, GRID_W)
        qstart = pl.multiple_of(j * GRID_W, GRID_W)
        for p in range(NA_HEADS // 2):
            ls = slice(p * LANES, (p + 1) * LANES)
            qp = q_ref[pl.ds(qstart, GRID_W), ls]
            kw = k_ref[pl.ds(kstart, win), ls]
            vw = v_ref[pl.ds(kstart, win), ls]
            kcp = kc_ref[:, ls]
            vcp = vc_ref[:, ls]
            outs = []
            for par in range(2):
                qm = jnp.where(lo if par == 0 else jnp.logical_not(lo), qp, jnp.zeros_like(qp))
                dr0 = (NA_ROWS - 1) - sh
                bias = jnp.concatenate([bias_ref[2 * p + par, dr0 + 2 * jj] for jj in range(NA_ROWS // 2)], axis=1)
                s_w = lax.dot_general(qm, kw, _NT, preferred_element_type=f32) + bias
                s_c = lax.dot_general(qm, kcp, _NT, preferred_element_type=f32)
                outs.append(_softmax_pv([(s_w, vw), (s_c, vcp)]))
            o_ref[pl.ds(qstart, GRID_W), ls] = jnp.where(lo, outs[0], outs[1])
        return carry

    lax.fori_loop(0, rb, row_body, 0)


def _bias_pairs_kernel(rv_ref, oh_ref, mask_ref, o_ref):
    acc = None
    for piece in _split3(rv_ref[...]):
        d = jnp.dot(piece, oh_ref[...], preferred_element_type=f32)
        acc = d if acc is None else acc + d
    o_ref[...] = acc + mask_ref[...]


def _na_bias_table(rpb_l):
    nh, ndr, ndc = rpb_l.shape
    pad = 32
    cols = np.arange(GRID_W)
    dc = np.clip(cols[None, :] - cols[:, None], -(NA_COLS - 1), NA_COLS - 1) + (NA_COLS - 1)
    col_start = np.clip(cols - NA_COLS // 2, 0, GRID_W - NA_COLS)
    col_in = (cols[None, :] >= col_start[:, None]) & (cols[None, :] < col_start[:, None] + NA_COLS)
    oh = np.zeros((2, pad, GRID_W, 2, GRID_W), np.float32)
    for s in range(2):
        oh[s, dc, cols[:, None], s, cols[None, :]] = 1.0
    oh = jnp.asarray(oh.reshape(2 * pad, GRID_W * LANES), bf16)
    mask = np.where(np.broadcast_to(col_in[:, None, :], (GRID_W, 2, GRID_W)), 0.0, NEG_INF)
    mask = jnp.asarray(mask.reshape(1, GRID_W * LANES), f32)
    rp = jnp.pad(rpb_l, ((0, 0), (0, 0), (0, pad - ndc)))
    rv = jnp.concatenate([rp[:, 0:ndr - 1], rp[:, 1:ndr]], axis=-1).reshape(nh * (ndr - 1), 2 * pad)
    tn = 2048
    out = pl.pallas_call(
        _bias_pairs_kernel, grid=(GRID_W * LANES // tn,),
        in_specs=[pl.BlockSpec(rv.shape, lambda j: (0, 0)),
                  pl.BlockSpec((2 * pad, tn), lambda j: (0, j)),
                  pl.BlockSpec((1, tn), lambda j: (0, j))],
        out_specs=pl.BlockSpec((rv.shape[0], tn), lambda j: (0, j)),
        out_shape=jax.ShapeDtypeStruct((rv.shape[0], GRID_W * LANES), f32),
        compiler_params=_cparams(("parallel",)), name="na_bias_table",
    )(rv, oh, mask)
    return out.reshape(nh, ndr - 1, GRID_W, LANES)


def _neighbourhood_attention(q, k, v, kc, vc, bias):
    b, n, w = q.shape
    l = kc.shape[1]
    rows = n // GRID_W
    rb = 8
    full = lambda m: pl.BlockSpec((None, m, w), lambda bi, i: (bi, 0, 0))
    return pl.pallas_call(
        functools.partial(_na_kernel, rows=rows, rb=rb),
        grid=(b, rows // rb),
        in_specs=[pl.BlockSpec((None, rb * GRID_W, w), lambda bi, i: (bi, i, 0)),
                  full(n), full(n), full(l), full(l),
                  pl.BlockSpec(bias.shape, lambda bi, i: (0, 0, 0, 0))],
        out_specs=pl.BlockSpec((None, rb * GRID_W, w), lambda bi, i: (bi, i, 0)),
        out_shape=jax.ShapeDtypeStruct((b, n, w), f32),
        compiler_params=_cparams(("parallel", "arbitrary"), 48 * 1024 * 1024),
        name="neighbourhood_attention",
    )(q, k, v, kc, vc, bias)


def _ctx_attn_kernel(q_ref, k_ref, v_ref, o_ref):
    m = q_ref.shape[0]
    lo = lax.broadcasted_iota(i32, (m, LANES), 1) < HEAD_DIM
    for p in range(NA_HEADS // 2):
        ls = slice(p * LANES, (p + 1) * LANES)
        qp, kp, vp = q_ref[:, ls], k_ref[:, ls], v_ref[:, ls]
        outs = []
        for par in range(2):
            qm = jnp.where(lo if par == 0 else jnp.logical_not(lo), qp, jnp.zeros_like(qp))
            s = lax.dot_general(qm, kp, _NT, preferred_element_type=f32)
            outs.append(_softmax_pv([(s, vp)]))
        o_ref[:, ls] = jnp.where(lo, outs[0], outs[1])


def _context_attention(q, k, v):
    b, l, w = q.shape
    spec = pl.BlockSpec((None, l, w), lambda bi: (bi, 0, 0))
    return pl.pallas_call(
        _ctx_attn_kernel, grid=(b,), in_specs=[spec, spec, spec], out_specs=spec,
        out_shape=jax.ShapeDtypeStruct((b, l, w), f32),
        compiler_params=_cparams(("parallel",)), name="context_attention",
    )(q, k, v)


def _dft_tables_channel():
    c = np.arange(FOURIER_GROUP_DIM)
    ang = 2.0 * np.pi * np.outer(c, c) / FOURIER_GROUP_DIM
    g = FOURIER_WIDTH // FOURIER_GROUP_DIM
    cb = np.kron(np.eye(g), np.cos(ang))
    sb = np.kron(np.eye(g), np.sin(ang))
    return np.concatenate([cb, -sb], axis=1)


def _fourier_kernel(f_ref, cs_ref, d1_ref, d3_ref, tc_ref, ts_ref, o_ref, g_scr, br_scr, bi_scr, o_scr, *, r):
    nt = FOURIER_WIDTH // LANES
    g = jnp.dot(f_ref[...], cs_ref[...].astype(bf16), preferred_element_type=f32)
    for c in range(2 * nt):
        g_scr[c] = g[:, c * LANES:(c + 1) * LANES]
    d1 = d1_ref[...].astype(bf16)

    def strided(scr, tiles, start, size, stride):
        return jnp.concatenate([scr[c, pl.ds(start, size, stride=stride), :] for c in tiles], axis=1)

    for n2 in range(GRID_W):
        gr = strided(g_scr, range(nt), n2, r, GRID_W)
        gi = strided(g_scr, range(nt, 2 * nt), n2, r, GRID_W)
        rhs = jnp.concatenate([gr, gi], axis=0).astype(bf16)
        a = jnp.dot(d1, rhs, preferred_element_type=f32)
        ar, ai = a[:r], a[r:]
        tc = tc_ref[n2 * r:(n2 + 1) * r, :]
        ts = ts_ref[n2 * r:(n2 + 1) * r, :]
        br = ar * tc + ai * ts
        bi = ai * tc - ar * ts
        for c in range(nt):
            br_scr[c, n2 * r:(n2 + 1) * r, :] = br[:, c * LANES:(c + 1) * LANES]
            bi_scr[c, n2 * r:(n2 + 1) * r, :] = bi[:, c * LANES:(c + 1) * LANES]
    d3 = d3_ref[...].astype(bf16)
    for k1 in range(r):
        br = strided(br_scr, range(nt), k1, GRID_W, r)
        bi = strided(bi_scr, range(nt), k1, GRID_W, r)
        rhs = jnp.concatenate([br, bi], axis=0).astype(bf16)
        y = jnp.dot(d3, rhs, preferred_element_type=f32)
        for c in range(nt):
            o_scr[c, pl.ds(k1, GRID_W, stride=r), :] = y[:, c * LANES:(c + 1) * LANES]
    for c in range(nt):
        o_ref[:, c * LANES:(c + 1) * LANES] = o_scr[c]


def _fourier_mix(f):
    b, n, w = f.shape
    r = n // GRID_W
    cs = jnp.asarray(_dft_tables_channel(), f32)
    a1 = 2.0 * np.pi * np.outer(np.arange(r), np.arange(r)) / r
    c1, s1 = np.cos(a1), np.sin(a1)
    d1 = jnp.asarray(np.block([[c1, s1], [-s1, c1]]), f32)
    a3 = 2.0 * np.pi * np.outer(np.arange(GRID_W), np.arange(GRID_W)) / GRID_W
    d3 = jnp.asarray(np.concatenate([np.cos(a3), np.sin(a3)], axis=1), f32)
    n2 = np.arange(GRID_W)[:, None]
    k1 = np.arange(r)[None, :]
    at = 2.0 * np.pi * ((n2 * k1) % n) / n
    scale = 1.0 / math.sqrt(n * FOURIER_GROUP_DIM)
    tc = jnp.asarray(np.broadcast_to((np.cos(at) * scale).reshape(n, 1), (n, w)), f32)
    ts = jnp.asarray(np.broadcast_to((np.sin(at) * scale).reshape(n, 1), (n, w)), f32)
    const = lambda a: pl.BlockSpec(a.shape, lambda bi: (0,) * a.ndim)
    return pl.pallas_call(
        functools.partial(_fourier_kernel, r=r),
        grid=(b,),
        in_specs=[pl.BlockSpec((None, n, w), lambda bi: (bi, 0, 0)),
                  const(cs), const(d1), const(d3), const(tc), const(ts)],
        out_specs=pl.BlockSpec((None, n, w), lambda bi: (bi, 0, 0)),
        out_shape=jax.ShapeDtypeStruct((b, n, w), f32),
        scratch_shapes=[pltpu.VMEM((2 * w // LANES, n, LANES), f32)] + [pltpu.VMEM((w // LANES, n, LANES), f32)] * 3,
        compiler_params=_cparams(("parallel",), 48 * 1024 * 1024),
        name="fourier_mix",
    )(f, cs, d1, d3, tc, ts)


def _fourier_direct_kernel(f_ref, cs_ref, dn_ref, o_ref):
    w = FOURIER_WIDTH
    g = jnp.dot(f_ref[...], cs_ref[...].astype(bf16), preferred_element_type=f32)
    rhs = jnp.concatenate([g[:, :w], g[:, w:]], axis=0).astype(bf16)
    o_ref[...] = jnp.dot(dn_ref[...].astype(bf16), rhs, preferred_element_type=f32)


def _fourier_mix_direct(f):
    b, n, w = f.shape
    cs = jnp.asarray(_dft_tables_channel(), f32)
    a = 2.0 * np.pi * (np.outer(np.arange(n), np.arange(n)) % n) / n
    scale = 1.0 / math.sqrt(n * FOURIER_GROUP_DIM)
    dn = jnp.asarray(np.concatenate([np.cos(a), np.sin(a)], axis=1) * scale, f32)
    const = lambda t: pl.BlockSpec(t.shape, lambda bi: (0,) * t.ndim)
    return pl.pallas_call(
        _fourier_direct_kernel, grid=(b,),
        in_specs=[pl.BlockSpec((None, n, w), lambda bi: (bi, 0, 0)), const(cs), const(dn)],
        out_specs=pl.BlockSpec((None, n, w), lambda bi: (bi, 0, 0)),
        out_shape=jax.ShapeDtypeStruct((b, n, w), f32),
        compiler_params=_cparams(("parallel",)), name="fourier_mix_direct",
    )(f, cs, dn)


def _to_rows(t):
    st = jnp.stack([t[:, c * LANES:(c + 1) * LANES] for c in range(ROW_TILES)], axis=0)
    return jnp.swapaxes(st, 0, 1)


def _from_rows(rows):
    t = jnp.swapaxes(rows, 0, 1)
    return [t[c] for c in range(ROW_TILES)]


def _merge_kernel(na_ref, bg_ref, z_ref, zp_ref, zn_ref, fo_ref, x_ref, cw_ref, gmix_ref, wout_ref,
                  gpost_ref, g1_ref, sh2_ref, sc2_ref, gpre_ref, wr_ref,
                  xo_ref, h2_ref, lg_ref):
    i = pl.program_id(1)
    nt = pl.num_programs(1)
    tm = z_ref.shape[0]
    z = z_ref[...]
    zp = jnp.where(i > 0, zp_ref[SUBLANES - 1:SUBLANES, :], 0.0)
    zn = jnp.where(i < nt - 1, zn_ref[0:1, :], 0.0)
    rid = lax.broadcasted_iota(i32, z.shape, 0)
    z_dn = jnp.where(rid == 0, zp, pltpu.roll(z, 1, axis=0))
    z_up = jnp.where(rid == tm - 1, zn, pltpu.roll(z, tm - 1, axis=0))
    cv = bg_ref[...] * (cw_ref[0:1, :] * z_dn + cw_ref[1:2, :] * z + cw_ref[2:3, :] * z_up)

    a, b_ = NA_WIDTH, NA_WIDTH + CONV_CH
    parts = ((na_ref[...], 0, a), (cv, a, b_), (fo_ref[...], b_, D_MODEL))
    m = None
    for t, lo, hi in parts:
        y = (_rms(t) * gmix_ref[:, lo:hi]).astype(bf16)
        d = jnp.dot(y, wout_ref[lo:hi, :], preferred_element_type=f32)
        m = d if m is None else m + d
    xn = x_ref[...] + g1_ref[...] * (_rms(m) * gpost_ref[...])
    xo_ref[...] = xn
    h2 = (_rms(xn) * gpre_ref[...]) * (1.0 + sc2_ref[...]) + sh2_ref[...]
    h2_ref[...] = _to_rows(h2)
    hh = h2.astype(bf16)
    hl = (h2 - hh.astype(f32)).astype(bf16)
    wr = wr_ref[...]
    wh = wr.astype(bf16)
    wl = (wr - wh.astype(f32)).astype(bf16)
    nt_dot = lambda u, v: lax.dot_general(u, v, _NT, preferred_element_type=f32)
    lg_ref[...] = nt_dot(wh, hh) + (nt_dot(wh, hl) + nt_dot(wl, hh))


def _merge(na, bg, z, fo, x, modl, mod_row, conv_w, g_mix, w_out_b, g_post, g_pre_ffn, w_router_t, tm):
    b, n, d = x.shape
    hb = tm // SUBLANES
    nhb = n // SUBLANES
    row = lambda c: pl.BlockSpec((None, None, 1, d), lambda bi, i: (mod_row(bi), c, 0, 0))
    tok = lambda w: pl.BlockSpec((None, tm, w), lambda bi, i: (bi, i, 0))
    const = lambda a: pl.BlockSpec(a.shape, lambda bi, i: (0,) * a.ndim)
    vec = lambda a: a.reshape(1, -1)
    nt = n // tm
    args = (na, bg, z, z, z, fo, x, conv_w, vec(g_mix), w_out_b, vec(g_post), modl, modl, modl,
            vec(g_pre_ffn), w_router_t)
    return pl.pallas_call(
        _merge_kernel,
        grid=(b, nt),
        in_specs=[tok(NA_WIDTH), tok(CONV_CH), tok(CONV_CH),
                  pl.BlockSpec((None, SUBLANES, CONV_CH), lambda bi, i: (bi, jnp.maximum(i * hb - 1, 0), 0)),
                  pl.BlockSpec((None, SUBLANES, CONV_CH), lambda bi, i: (bi, jnp.minimum((i + 1) * hb, nhb - 1), 0)),
                  tok(FOURIER_WIDTH), tok(d), const(conv_w), const(vec(g_mix)), const(w_out_b),
                  const(vec(g_post)), row(2), row(3), row(4), const(vec(g_pre_ffn)), const(w_router_t)],
        out_specs=[tok(d),
                   pl.BlockSpec((tm, ROW_TILES, LANES), lambda bi, i: (bi * nt + i, 0, 0)),
                   pl.BlockSpec((None, N_EXPERTS, tm), lambda bi, i: (bi, 0, i))],
        out_shape=[jax.ShapeDtypeStruct((b, n, d), f32),
                   jax.ShapeDtypeStruct((b * n, ROW_TILES, LANES), f32),
                   jax.ShapeDtypeStruct((b, N_EXPERTS, n), f32)],
        compiler_params=_cparams(("parallel", "parallel"), 40 * 1024 * 1024),
        name="merge",
    )(*args)


def _split3(a):
    h = a.astype(bf16)
    r1 = a - h.astype(f32)
    m = r1.astype(bf16)
    l = (r1 - m.astype(f32)).astype(bf16)
    return h, m, l


def _threshold_search(aff, count, cap):
    t = jnp.zeros(aff.shape, i32)
    for bit in range(30, -1, -1):
        cand = t | (1 << bit)
        t = jnp.where(count(aff >= lax.bitcast_convert_type(cand, f32)) >= cap, cand, t)
    return lax.bitcast_convert_type(t, f32)


def _route_kernel(lg_ref, o_ref, *, nch, cap):
    ne = N_EXPERTS
    r = ne * nch
    sh = nch.bit_length() - 1
    lg = lg_ref[...]
    ex = jnp.exp(lg - jnp.max(lg, axis=0, keepdims=True))
    aff = (ex / jnp.sum(ex, axis=0, keepdims=True)).reshape(r, LANES)

    li = lax.broadcasted_iota(i32, (LANES, LANES), 0)
    lj = lax.broadcasted_iota(i32, (LANES, LANES), 1)
    triu = _ones_where(li <= lj)
    tril = _ones_where(lj <= li)
    eye = _ones_where(li == lj)
    ones = jnp.ones((LANES, LANES), bf16)
    ri = lax.broadcasted_iota(i32, (r, r), 0)
    rj = lax.broadcasted_iota(i32, (r, r), 1)
    same = (ri >> sh) == (rj >> sh)
    bd = _ones_where(same)
    sl = _ones_where(same & (rj < ri))
    mm = lambda u, v: jnp.dot(u, v, preferred_element_type=f32)

    def count(mask):
        return mm(bd, mm(_ones_where(mask), ones).astype(bf16))

    thr = _threshold_search(aff, count, cap)
    gt = aff > thr
    eq = aff == thr
    need = cap - count(gt)
    eqb = _ones_where(eq)
    pe = mm(eqb, triu) + mm(sl, mm(eqb, ones).astype(bf16))
    sel = gt | (eq & (pe <= need))
    selb = _ones_where(sel)
    rt = mm(selb, ones)
    oin = mm(sl, rt.astype(bf16)) + rt

    reps = cap // LANES
    tile = lambda a: jnp.concatenate([a] * reps, axis=1) if reps > 1 else a
    s_f = lax.broadcasted_iota(i32, (nch, cap), 1).astype(f32)
    s8 = lax.broadcasted_iota(i32, (SUBLANES, cap), 1).astype(f32)
    ji = lax.broadcasted_iota(i32, (nch, cap), 0).astype(f32)
    lis = lax.broadcasted_iota(i32, (LANES, cap), 0).astype(f32)
    ri8 = lax.broadcasted_iota(i32, (SUBLANES, cap), 0)
    ones8c = jnp.ones((SUBLANES, nch), bf16)
    ones8l = jnp.ones((SUBLANES, LANES), bf16)
    for e in range(ne):
        rows = slice(e * nch, (e + 1) * nch)
        zmask = tile(oin[rows]) <= s_f
        js = mm(ones8c, _ones_where(zmask))
        oex = mm(ones8c, jnp.where(zmask, tile(rt[rows]), 0.0).astype(bf16))
        s_loc = s8 - oex
        u = _ones_where(ji == jnp.broadcast_to(js[0:1], (nch, cap)))
        plt = lax.dot_general(tril, selb[rows], _NT, preferred_element_type=f32)
        ptg = mm(plt.astype(bf16), u)
        c = _ones_where(ptg <= jnp.broadcast_to(s_loc[0:1], (LANES, cap)))
        tl = mm(ones8l, c)
        idx = js * float(LANES) + tl
        affg = None
        for piece in _split3(aff[rows]):
            pt = lax.dot_general(eye, piece, _NT, preferred_element_type=f32).astype(bf16)
            pg = mm(pt, u)
            affg = pg if affg is None else affg + pg
        hit = lis == jnp.broadcast_to(tl[0:1], (LANES, cap))
        gate = jnp.sum(jnp.where(hit, affg, 0.0), axis=0, keepdims=True)
        o_ref[e] = jnp.where(ri8 == 0, idx, jnp.where(ri8 == 1, jnp.broadcast_to(gate, (SUBLANES, cap)), 0.0))


def _route_small_kernel(lg_ref, o_ref, *, cap):
    ne, n = lg_ref.shape
    lg = lg_ref[...]
    ex = jnp.exp(lg - jnp.max(lg, axis=0, keepdims=True))
    aff = ex / jnp.sum(ex, axis=0, keepdims=True)
    li = lax.broadcasted_iota(i32, (n, n), 0)
    lj = lax.broadcasted_iota(i32, (n, n), 1)
    triu = _ones_where(li <= lj)
    ones = jnp.ones((n, n), bf16)
    mm = lambda u, v: jnp.dot(u, v, preferred_element_type=f32)
    count = lambda mask: mm(_ones_where(mask), ones)
    thr = _threshold_search(aff, count, cap)
    gt = aff > thr
    eq = aff == thr
    need = cap - count(gt)
    sel = gt | (eq & (mm(_ones_where(eq), triu) <= need))
    slot = jnp.where(sel, mm(_ones_where(sel), triu) - 1.0, -1.0)
    tid = lax.broadcasted_iota(i32, (SUBLANES, n), 1).astype(f32)
    ri = lax.broadcasted_iota(i32, (SUBLANES, n), 0)
    s_i = lax.broadcasted_iota(i32, (cap, n), 0).astype(f32)
    ro = lax.broadcasted_iota(i32, (SUBLANES, cap), 0)
    bc = lambda a: jnp.broadcast_to(a, (SUBLANES, n))
    for e in range(ne):
        et = _ones_where(s_i == jnp.broadcast_to(slot[e:e + 1], (cap, n)))
        h, m, l = _split3(aff[e:e + 1])
        rm = jnp.where(ri == 0, tid, jnp.where(ri == 1, bc(h.astype(f32)), jnp.where(
            ri == 2, bc(m.astype(f32)), jnp.where(ri == 3, bc(l.astype(f32)), 0.0)))).astype(bf16)
        res = lax.dot_general(rm, et, _NT, preferred_element_type=f32)
        gate = res[1:2] + res[2:3] + res[3:4]
        o_ref[e] = jnp.where(ro == 0, jnp.broadcast_to(res[0:1], (SUBLANES, cap)),
                             jnp.where(ro == 1, jnp.broadcast_to(gate, (SUBLANES, cap)), 0.0))


def _route(logits_t):
    b, ne, n = logits_t.shape
    cap = CAPACITY_FACTOR * n // N_EXPERTS
    out_shape = jax.ShapeDtypeStruct((b, ne, SUBLANES, cap), f32)
    out_spec = pl.BlockSpec((None, ne, SUBLANES, cap), lambda bi: (bi, 0, 0, 0))
    nch = n // LANES
    if nch % SUBLANES == 0 and cap % LANES == 0:
        res = pl.pallas_call(
            functools.partial(_route_kernel, nch=nch, cap=cap), grid=(b,),
            in_specs=[pl.BlockSpec((None, ne, nch, LANES), lambda bi: (bi, 0, 0, 0))],
            out_specs=out_spec, out_shape=out_shape,
            compiler_params=_cparams(("parallel",), 40 * 1024 * 1024), name="route",
        )(logits_t.reshape(b, ne, nch, LANES))
    else:
        res = pl.pallas_call(
            functools.partial(_route_small_kernel, cap=cap), grid=(b,),
            in_specs=[pl.BlockSpec((None, ne, n), lambda bi: (bi, 0, 0))],
            out_specs=out_spec, out_shape=out_shape,
            compiler_params=_cparams(("parallel",)), name="route_small",
        )(logits_t)
    idx = res[:, :, 0, :].astype(i32).reshape(b * ne, 1, cap)
    gates = res[:, :, 1, :].reshape(b * ne, 1, cap)
    return idx, gates


def _dispatch_kernel(idx_ref, h_ref, o_ref, scr):
    cap = o_ref.shape[0]

    def body(i, c):
        scr[i] = h_ref[idx_ref[0, 0, i]]
        return c

    lax.fori_loop(0, cap, body, 0, unroll=8)
    tiles = _from_rows(scr[...])
    for c in range(ROW_TILES):
        o_ref[:, c * LANES:(c + 1) * LANES] = tiles[c].astype(bf16)


def _dispatch(h2_rows, idx, b):
    n = h2_rows.shape[0] // b
    cap = idx.shape[2]
    ne = N_EXPERTS
    return pl.pallas_call(
        _dispatch_kernel,
        grid=(b, ne),
        in_specs=[pl.BlockSpec((1, 1, cap), lambda bi, e: (bi * ne + e, 0, 0), memory_space=pltpu.SMEM),
                  pl.BlockSpec((n, ROW_TILES, LANES), lambda bi, e: (bi, 0, 0))],
        out_specs=pl.BlockSpec((None, cap, D_MODEL), lambda bi, e: (e, bi, 0)),
        out_shape=jax.ShapeDtypeStruct((ne, b * cap, D_MODEL), bf16),
        scratch_shapes=[pltpu.VMEM((cap, ROW_TILES, LANES), f32)],
        compiler_params=_cparams(("parallel", "arbitrary"), 48 * 1024 * 1024),
        name="dispatch",
    )(idx, h2_rows)


def _combine_kernel(idx_ref, g_ref, y_ref, o_ref, acc, ys, *, chunk):
    s = pl.program_id(1)
    ne = N_EXPERTS
    cap = y_ref.shape[0]

    @pl.when(s == 0)
    def _():
        acc[...] = jnp.zeros_like(acc)

    @pl.when(s < ne)
    def _():
        ys[...] = _to_rows(y_ref[...].astype(f32))

        def body(i, c):
            t = idx_ref[0, 0, i]
            acc[t] = acc[t] + g_ref[0, 0, i] * ys[i]
            return c

        lax.fori_loop(0, cap, body, 0, unroll=8)

    @pl.when(s >= ne)
    def _():
        start = pl.multiple_of((s - ne) * chunk, chunk)
        tiles = _from_rows(acc[pl.ds(start, chunk)])
        for c in range(ROW_TILES):
            o_ref[:, c * LANES:(c + 1) * LANES] = tiles[c]


def _combine(y, idx, gates, b, n):
    ne = N_EXPERTS
    cap = idx.shape[2]
    chunk = min(n, 512)
    nchunk = n // chunk
    last = ne - 1
    sm = lambda: pl.BlockSpec((1, 1, cap), lambda bi, s: (bi * ne + jnp.minimum(s, last), 0, 0),
                              memory_space=pltpu.SMEM)
    return pl.pallas_call(
        functools.partial(_combine_kernel, chunk=chunk),
        grid=(b, ne + nchunk),
        in_specs=[sm(), sm(),
                  pl.BlockSpec((None, cap, D_MODEL), lambda bi, s: (jnp.minimum(s, last), bi, 0))],
        out_specs=pl.BlockSpec((None, chunk, D_MODEL), lambda bi, s: (bi, jnp.maximum(s - ne, 0), 0)),
        out_shape=jax.ShapeDtypeStruct((b, n, D_MODEL), f32),
        scratch_shapes=[pltpu.VMEM((n, ROW_TILES, LANES), f32), pltpu.VMEM((cap, ROW_TILES, LANES), f32)],
        compiler_params=_cparams(("parallel", "arbitrary"), 48 * 1024 * 1024),
        name="combine",
    )(idx, gates, y)


def _ffn_kernel(*refs, nseg, nf, tf):
    xs = refs[:nseg]
    wg_ref, wu_ref, wd_ref = refs[nseg:nseg + 3]
    ys = refs[nseg + 3:2 * nseg + 3]
    hs = refs[2 * nseg + 3:]
    s = pl.program_id(1)

    @pl.when(s < nf)
    def _():
        wg = wg_ref[...].astype(bf16)
        wu = wu_ref[...].astype(bf16)
        for x_ref, h_scr in zip(xs, hs):
            x = x_ref[...]
            g = jnp.dot(x, wg, preferred_element_type=f32)
            u = jnp.dot(x, wu, preferred_element_type=f32)
            h = ((g * jax.nn.sigmoid(g)) * u).astype(bf16)
            for f in range(nf):
                @pl.when(s == f)
                def _(f=f, h=h, h_scr=h_scr):
                    h_scr[:, f * tf:(f + 1) * tf] = h

    @pl.when(s >= nf)
    def _():
        wd = wd_ref[...].astype(bf16)
        for y_ref, h_scr in zip(ys, hs):
            y_ref[...] = jnp.dot(h_scr[...], wd, preferred_element_type=f32)


def _expert_ffn(xs_list, w_gate, w_up, w_down, layer):
    _, ne, d, fh = w_gate.shape
    tf = 256
    tn = 256
    nf = fh // tf
    nd = d // tn
    nseg = len(xs_list)
    xspec = lambda m: pl.BlockSpec((None, m, d), lambda e, s: (e, 0, 0))
    wspec = pl.BlockSpec((None, None, d, tf), lambda e, s: (layer, e, 0, jnp.minimum(s, nf - 1)))
    dspec = pl.BlockSpec((None, None, fh, tn), lambda e, s: (layer, e, 0, jnp.maximum(s - nf, 0)))
    yspec = lambda m: pl.BlockSpec((None, m, tn), lambda e, s: (e, 0, jnp.maximum(s - nf, 0)))
    ms = [x.shape[1] for x in xs_list]
    return pl.pallas_call(
        functools.partial(_ffn_kernel, nseg=nseg, nf=nf, tf=tf),
        grid=(ne, nf + nd),
        in_specs=[xspec(m) for m in ms] + [wspec, wspec, dspec],
        out_specs=[yspec(m) for m in ms],
        out_shape=[jax.ShapeDtypeStruct((ne, m, d), f32) for m in ms],
        scratch_shapes=[pltpu.VMEM((m, fh), bf16) for m in ms],
        compiler_params=_cparams(("parallel", "arbitrary"), VMEM_LIMIT),
        name="expert_ffn",
    )(*xs_list, w_gate, w_up, w_down)


def _post_kernel(x_ref, y_ref, g2_ref, gp_ref, o_ref):
    o_ref[...] = x_ref[...] + g2_ref[...] * (_rms(y_ref[...]) * gp_ref[...])


def _post(x, y, modl, mod_row, g_post, tm):
    b, n, d = x.shape
    tok = pl.BlockSpec((None, tm, d), lambda bi, i: (bi, i, 0))
    return pl.pallas_call(
        _post_kernel, grid=(b, n // tm),
        in_specs=[tok, tok, pl.BlockSpec((None, None, 1, d), lambda bi, i: (mod_row(bi), 5, 0, 0)),
                  pl.BlockSpec((1, d), lambda bi, i: (0, 0))],
        out_specs=tok, out_shape=jax.ShapeDtypeStruct((b, n, d), f32),
        compiler_params=_cparams(("parallel", "parallel")), name="post_ffn",
    )(x, y, modl, g_post.reshape(1, d))


def kernel(x, c, ctx, c_ctx, w_mod, b_mod, g_pre_mix, g_post_mix, g_pre_ffn, g_post_ffn, w_in, rpb,
           conv_w, g_mix_out, w_out, w_router, w_gate, w_up, w_down):
    depth = w_mod.shape[0]
    b, n, d = x.shape
    l_ctx = ctx.shape[1]
    ctx_row = b
    cvec = jnp.concatenate([c, c_ctx[None, :], jnp.zeros((SUBLANES - b - 1, d), f32)], axis=0)
    mod = _modulation(cvec, w_mod, b_mod)
    lat_row = lambda bi: bi
    cx_row = lambda bi: ctx_row
    tm = min(n, 512)
    for layer in range(depth):
        last = layer == depth - 1
        modl = mod[layer].reshape(SUBLANES, N_MOD, 1, d)
        w_in_b = w_in[layer].astype(bf16)
        w_out_b = w_out[layer].astype(bf16)
        w_router_t = w_router[layer].T
        q, k, v, bg, z, f = _in_projection(x, modl, lat_row, g_pre_mix[layer], w_in_b, tm)
        qc, kc, vc, bgc, zc, fc = _in_projection(ctx, modl, cx_row, g_pre_mix[layer], w_in_b, l_ctx)
        na = _neighbourhood_attention(q, k, v, kc, vc, _na_bias_table(rpb[layer]))
        fo = _fourier_mix(f)
        x, h2_rows, logits = _merge(na, bg, z, fo, x, modl, lat_row, conv_w[layer], g_mix_out[layer],
                                    w_out_b, g_post_mix[layer], g_pre_ffn[layer], w_router_t, tm)
        idx, gates = _route(logits)
        xs_list = [_dispatch(h2_rows, idx, b)]
        if not last:
            na_c = _context_attention(qc, kc, vc)
            fo_c = _fourier_mix_direct(fc)
            ctx, h2c_rows, logits_c = _merge(na_c, bgc, zc, fo_c, ctx, modl, cx_row, conv_w[layer],
                                             g_mix_out[layer], w_out_b, g_post_mix[layer],
                                             g_pre_ffn[layer], w_router_t, l_ctx)
            idx_c, gates_c = _route(logits_c)
            xs_list.append(_dispatch(h2c_rows, idx_c, b))
        ys = _expert_ffn(xs_list, w_gate, w_up, w_down, layer)
        x = _post(x, _combine(ys[0], idx, gates, b, n), modl, lat_row, g_post_ffn[layer], tm)
        if not last:
            ctx = _post(ctx, _combine(ys[1], idx_c, gates_c, b, l_ctx), modl, cx_row, g_post_ffn[layer], l_ctx)
    return x
```

```python
import functools
import math

import numpy as np
import jax
import jax.numpy as jnp
from jax import lax
from jax.experimental import pallas as pl
from jax.experimental.pallas import tpu as pltpu

f32 = jnp.float32
bf16 = jnp.bfloat16
i32 = jnp.int32

D_MODEL = 1024
GRID_W = 64
HEAD_DIM = 64
NA_WIDTH = 512
NA_HEADS = 8
NA_ROWS = 8
NA_COLS = 16
CONV_CH = 256
FOURIER_WIDTH = 256
FOURIER_GROUP_DIM = 64
N_EXPERTS = 16
CAPACITY_FACTOR = 2
N_MOD = 6
RMS_EPS = 1e-6
NEG_INF = -1e30

LANES = 128
SUBLANES = 8
ROW_TILES = D_MODEL // LANES
VMEM_LIMIT = 56 * 1024 * 1024

_NT = (((1,), (1,)), ((), ()))


def _cparams(sem, vmem=None):
    return pltpu.CompilerParams(dimension_semantics=sem, vmem_limit_bytes=vmem)


def _rms(x):
    return x * lax.rsqrt(jnp.mean(x * x, axis=-1, keepdims=True) + RMS_EPS)


def _ones_where(mask, dtype=bf16):
    return jnp.where(mask, 1.0, 0.0).astype(dtype)


def _mod_kernel(c_ref, w_ref, b_ref, o_ref):
    c = c_ref[...]
    s = c * jax.nn.sigmoid(c)
    o_ref[...] = jnp.dot(s, w_ref[...], preferred_element_type=f32,
                         precision=lax.Precision.HIGHEST) + b_ref[...]


def _modulation(cvec, w_mod, b_mod):
    depth, d, n6 = w_mod.shape
    tn = 1536
    return pl.pallas_call(
        _mod_kernel,
        grid=(depth, n6 // tn),
        in_specs=[
            pl.BlockSpec((SUBLANES, d), lambda l, j: (0, 0)),
            pl.BlockSpec((None, d, tn), lambda l, j: (l, 0, j)),
            pl.BlockSpec((None, 1, tn), lambda l, j: (l, 0, j)),
        ],
        out_specs=pl.BlockSpec((None, SUBLANES, tn), lambda l, j: (l, 0, j)),
        out_shape=jax.ShapeDtypeStruct((depth, SUBLANES, n6), f32),
        compiler_params=_cparams(("parallel", "parallel"), 40 * 1024 * 1024),
        name="modulation",
    )(cvec, w_mod, b_mod.reshape(depth, 1, n6))


def _inproj_kernel(x_ref, sh_ref, sc_ref, g_ref, w_ref, wvt_ref, q_ref, k_ref, v_ref, vt_ref, bg_ref, z_ref, f_ref):
    x = x_ref[...]
    h = (_rms(x) * g_ref[...]) * (1.0 + sc_ref[...]) + sh_ref[...]
    hb = h.astype(bf16)

    def proj(a, b):
        return jnp.dot(hb, w_ref[:, a:b], preferred_element_type=f32)

    o = 3 * NA_WIDTH
    q_ref[...] = (proj(0, NA_WIDTH) * (HEAD_DIM ** -0.5)).astype(bf16)
    k_ref[...] = proj(NA_WIDTH, 2 * NA_WIDTH).astype(bf16)
    v_ref[...] = proj(2 * NA_WIDTH, o).astype(bf16)
    vt_ref[...] = lax.dot_general(wvt_ref[...], hb, _NT, preferred_element_type=f32).astype(bf16)
    bg_ref[...] = proj(o, o + CONV_CH)
    z_ref[...] = proj(o + CONV_CH, o + 2 * CONV_CH) * proj(o + 2 * CONV_CH, o + 3 * CONV_CH)
    f_ref[...] = proj(o + 3 * CONV_CH, o + 3 * CONV_CH + FOURIER_WIDTH).astype(bf16)


def _in_projection(x, modl, mod_row, g_pre, w_in_b, tm):
    b, n, d = x.shape
    in_w = w_in_b.shape[1]
    wvt = w_in_b[:, 2 * NA_WIDTH:3 * NA_WIDTH].T
    row = lambda c: pl.BlockSpec((None, None, 1, d), lambda bi, i: (mod_row(bi), c, 0, 0))
    tok = lambda w: pl.BlockSpec((None, tm, w), lambda bi, i: (bi, i, 0))
    shp = lambda w, dt: jax.ShapeDtypeStruct((b, n, w), dt)
    return pl.pallas_call(
        _inproj_kernel,
        grid=(b, n // tm),
        in_specs=[tok(d), row(0), row(1),
                  pl.BlockSpec((1, d), lambda bi, i: (0, 0)),
                  pl.BlockSpec((d, in_w), lambda bi, i: (0, 0)),
                  pl.BlockSpec((NA_WIDTH, d), lambda bi, i: (0, 0))],
        out_specs=[tok(NA_WIDTH), tok(NA_WIDTH), tok(NA_WIDTH),
                   pl.BlockSpec((None, NA_WIDTH, tm), lambda bi, i: (bi, 0, i)),
                   tok(CONV_CH), tok(CONV_CH), tok(FOURIER_WIDTH)],
        out_shape=[shp(NA_WIDTH, bf16), shp(NA_WIDTH, bf16), shp(NA_WIDTH, bf16),
                   jax.ShapeDtypeStruct((b, NA_WIDTH, n), bf16),
                   shp(CONV_CH, f32), shp(CONV_CH, f32), shp(FOURIER_WIDTH, bf16)],
        compiler_params=_cparams(("parallel", "parallel"), 40 * 1024 * 1024),
        name="in_projection",
    )(x, modl, modl, g_pre.reshape(1, d), w_in_b, wvt)


def _softmax_pv(parts):
    m = None
    for s, _ in parts:
        mi = jnp.max(s, axis=-1, keepdims=True)
        m = mi if m is None else jnp.maximum(m, mi)
    l = None
    o = None
    for s, v in parts:
        e = jnp.exp(s - m)
        li = jnp.sum(e, axis=-1, keepdims=True)
        oi = jnp.dot(e.astype(bf16), v, preferred_element_type=f32)
        l = li if l is None else l + li
        o = oi if o is None else o + oi
    return o * (1.0 / l)


def _na_kernel(q_ref, k_ref, vt_ref, kc_ref, vct_ref, bias_ref, o_ref, w_scr, *, rows, rb):
    i = pl.program_id(1)
    win = NA_ROWS * GRID_W
    npair = NA_HEADS // 2
    nwin = rows - 1
    half = NA_ROWS // 2

    @pl.when(i == 0)
    def _():
        for p in range(npair):
            vt = vt_ref[p * LANES:(p + 1) * LANES, :]
            for j in range(nwin):
                w_scr[p * nwin + j] = vt[:, j * GRID_W:j * GRID_W + LANES]

    lo = lax.broadcasted_iota(i32, (GRID_W, LANES), 1) < HEAD_DIM

    def row_body(j, carry):
        r = i * rb + j
        rs = jnp.clip(r - half, 0, rows - NA_ROWS)
        dr0 = (NA_ROWS - 1) - (r - rs)
        kstart = pl.multiple_of(rs * GRID_W, GRID_W)
        qstart = pl.multiple_of(j * GRID_W, GRID_W)
        for p in range(npair):
            ls = slice(p * LANES, (p + 1) * LANES)
            qp = q_ref[pl.ds(qstart, GRID_W), ls]
            zero = jnp.zeros_like(qp)
            q2 = jnp.concatenate([jnp.where(lo, qp, zero), jnp.where(lo, zero, qp)], axis=0)
            s_w = lax.dot_general(k_ref[pl.ds(kstart, win), ls], q2, _NT, preferred_element_type=f32)
            s_c = lax.dot_general(kc_ref[:, ls], q2, _NT, preferred_element_type=f32)
            s_w = s_w + jnp.concatenate([bias_ref[p, dr0 + 2 * jj] for jj in range(half)], axis=0)
            m = jnp.maximum(jnp.max(s_w, axis=0, keepdims=True), jnp.max(s_c, axis=0, keepdims=True))
            e_w = jnp.exp(s_w - m)
            e_c = jnp.exp(s_c - m)
            l = jnp.sum(e_w, axis=0, keepdims=True) + jnp.sum(e_c, axis=0, keepdims=True)
            vtw = jnp.concatenate([w_scr[p * nwin + rs + 2 * jj] for jj in range(half)], axis=1)
            ot = (jnp.dot(vtw, e_w.astype(bf16), preferred_element_type=f32)
                  + jnp.dot(vct_ref[ls, :], e_c.astype(bf16), preferred_element_type=f32))
            of = (ot * (1.0 / l)).T
            o_ref[pl.ds(qstart, GRID_W), ls] = jnp.where(lo, of[:GRID_W], of[GRID_W:])
        return carry

    lax.fori_loop(0, rb, row_body, 0)


def _bias_pairs_kernel(rv_ref, oh_ref, mask_ref, o_ref):
    acc = None
    for piece in _split3(rv_ref[...]):
        d = jnp.dot(piece, oh_ref[...], preferred_element_type=f32)
        acc = d if acc is None else acc + d
    o_ref[...] = acc + mask_ref[...]


def _na_bias_table(rpb_l):
    nh, ndr, ndc = rpb_l.shape
    pad = 32
    cols = np.arange(GRID_W)
    dc = np.clip(cols[None, :] - cols[:, None], -(NA_COLS - 1), NA_COLS - 1) + (NA_COLS - 1)
    col_start = np.clip(cols - NA_COLS // 2, 0, GRID_W - NA_COLS)
    col_in = (cols[None, :] >= col_start[:, None]) & (cols[None, :] < col_start[:, None] + NA_COLS)
    oh = np.zeros((2, 2, pad, 2, GRID_W, 2, GRID_W), np.float32)
    for par in range(2):
        for s in range(2):
            oh[par, s, dc, s, cols[None, :], par, cols[:, None]] = 1.0
    ncol = 2 * GRID_W * LANES
    oh = jnp.asarray(oh.reshape(4 * pad, ncol), bf16)
    mask = np.where(np.broadcast_to(col_in.T[None, :, None, :], (2, GRID_W, 2, GRID_W)), 0.0, NEG_INF)
    mask = jnp.asarray(mask.reshape(1, ncol), f32)
    rp = jnp.pad(rpb_l, ((0, 0), (0, 0), (0, pad - ndc)))
    rv = jnp.concatenate([rp[:, 0:ndr - 1], rp[:, 1:ndr]], axis=-1)
    rv = rv.reshape(nh // 2, 2, ndr - 1, 2 * pad).transpose(0, 2, 1, 3).reshape(nh // 2 * (ndr - 1), 4 * pad)
    tn = 2048
    out = pl.pallas_call(
        _bias_pairs_kernel, grid=(ncol // tn,),
        in_specs=[pl.BlockSpec(rv.shape, lambda j: (0, 0)),
                  pl.BlockSpec((4 * pad, tn), lambda j: (0, j)),
                  pl.BlockSpec((1, tn), lambda j: (0, j))],
        out_specs=pl.BlockSpec((rv.shape[0], tn), lambda j: (0, j)),
        out_shape=jax.ShapeDtypeStruct((rv.shape[0], ncol), f32),
        compiler_params=_cparams(("parallel",)), name="na_bias_table",
    )(rv, oh, mask)
    return out.reshape(nh // 2, ndr - 1, LANES, LANES)


def _neighbourhood_attention(q, k, vt, kc, vct, bias):
    b, n, w = q.shape
    l = kc.shape[1]
    rows = n // GRID_W
    rb = 8
    return pl.pallas_call(
        functools.partial(_na_kernel, rows=rows, rb=rb),
        grid=(b, rows // rb),
        in_specs=[pl.BlockSpec((None, rb * GRID_W, w), lambda bi, i: (bi, i, 0)),
                  pl.BlockSpec((None, n, w), lambda bi, i: (bi, 0, 0)),
                  pl.BlockSpec((None, w, n), lambda bi, i: (bi, 0, 0)),
                  pl.BlockSpec((None, l, w), lambda bi, i: (bi, 0, 0)),
                  pl.BlockSpec((None, w, l), lambda bi, i: (bi, 0, 0)),
                  pl.BlockSpec(bias.shape, lambda bi, i: (0, 0, 0, 0))],
        out_specs=pl.BlockSpec((None, rb * GRID_W, w), lambda bi, i: (bi, i, 0)),
        out_shape=jax.ShapeDtypeStruct((b, n, w), f32),
        scratch_shapes=[pltpu.VMEM((NA_HEADS // 2 * (rows - 1), LANES, LANES), bf16)],
        compiler_params=_cparams(("parallel", "arbitrary"), 48 * 1024 * 1024),
        name="neighbourhood_attention",
    )(q, k, vt, kc, vct, bias)


def _ctx_attn_kernel(q_ref, k_ref, v_ref, o_ref):
    m = q_ref.shape[0]
    lo = lax.broadcasted_iota(i32, (m, LANES), 1) < HEAD_DIM
    for p in range(NA_HEADS // 2):
        ls = slice(p * LANES, (p + 1) * LANES)
        qp, kp, vp = q_ref[:, ls], k_ref[:, ls], v_ref[:, ls]
        outs = []
        for par in range(2):
            qm = jnp.where(lo if par == 0 else jnp.logical_not(lo), qp, jnp.zeros_like(qp))
            s = lax.dot_general(qm, kp, _NT, preferred_element_type=f32)
            outs.append(_softmax_pv([(s, vp)]))
        o_ref[:, ls] = jnp.where(lo, outs[0], outs[1])


def _context_attention(q, k, v):
    b, l, w = q.shape
    spec = pl.BlockSpec((None, l, w), lambda bi: (bi, 0, 0))
    return pl.pallas_call(
        _ctx_attn_kernel, grid=(b,), in_specs=[spec, spec, spec], out_specs=spec,
        out_shape=jax.ShapeDtypeStruct((b, l, w), f32),
        compiler_params=_cparams(("parallel",)), name="context_attention",
    )(q, k, v)


def _dft_tables_channel():
    c = np.arange(FOURIER_GROUP_DIM)
    ang = 2.0 * np.pi * np.outer(c, c) / FOURIER_GROUP_DIM
    g = FOURIER_WIDTH // FOURIER_GROUP_DIM
    cb = np.kron(np.eye(g), np.cos(ang))
    sb = np.kron(np.eye(g), np.sin(ang))
    return np.concatenate([cb, -sb], axis=1)


def _fourier_kernel(f_ref, cs_ref, d1_ref, d3_ref, tc_ref, ts_ref, o_ref, g_scr, br_scr, bi_scr, o_scr, *, r):
    nt = FOURIER_WIDTH // LANES
    g = jnp.dot(f_ref[...], cs_ref[...].astype(bf16), preferred_element_type=f32)
    for c in range(2 * nt):
        g_scr[c] = g[:, c * LANES:(c + 1) * LANES]
    d1 = d1_ref[...].astype(bf16)

    def strided(scr, tiles, start, size, stride):
        return jnp.concatenate([scr[c, pl.ds(start, size, stride=stride), :] for c in tiles], axis=1)

    for n2 in range(GRID_W):
        gr = strided(g_scr, range(nt), n2, r, GRID_W)
        gi = strided(g_scr, range(nt, 2 * nt), n2, r, GRID_W)
        rhs = jnp.concatenate([gr, gi], axis=0).astype(bf16)
        a = jnp.dot(d1, rhs, preferred_element_type=f32)
        ar, ai = a[:r], a[r:]
        tc = tc_ref[n2 * r:(n2 + 1) * r, :]
        ts = ts_ref[n2 * r:(n2 + 1) * r, :]
        br = ar * tc + ai * ts
        bi = ai * tc - ar * ts
        for c in range(nt):
            br_scr[c, n2 * r:(n2 + 1) * r, :] = br[:, c * LANES:(c + 1) * LANES]
            bi_scr[c, n2 * r:(n2 + 1) * r, :] = bi[:, c * LANES:(c + 1) * LANES]
    d3 = d3_ref[...].astype(bf16)
    for k1 in range(r):
        br = strided(br_scr, range(nt), k1, GRID_W, r)
        bi = strided(bi_scr, range(nt), k1, GRID_W, r)
        rhs = jnp.concatenate([br, bi], axis=0).astype(bf16)
        y = jnp.dot(d3, rhs, preferred_element_type=f32)
        for c in range(nt):
            o_scr[c, pl.ds(k1, GRID_W, stride=r), :] = y[:, c * LANES:(c + 1) * LANES]
    for c in range(nt):
        o_ref[:, c * LANES:(c + 1) * LANES] = o_scr[c]


def _fourier_mix(f):
    b, n, w = f.shape
    r = n // GRID_W
    cs = jnp.asarray(_dft_tables_channel(), f32)
    a1 = 2.0 * np.pi * np.outer(np.arange(r), np.arange(r)) / r
    c1, s1 = np.cos(a1), np.sin(a1)
    d1 = jnp.asarray(np.block([[c1, s1], [-s1, c1]]), f32)
    a3 = 2.0 * np.pi * np.outer(np.arange(GRID_W), np.arange(GRID_W)) / GRID_W
    d3 = jnp.asarray(np.concatenate([np.cos(a3), np.sin(a3)], axis=1), f32)
    n2 = np.arange(GRID_W)[:, None]
    k1 = np.arange(r)[None, :]
    at = 2.0 * np.pi * ((n2 * k1) % n) / n
    scale = 1.0 / math.sqrt(n * FOURIER_GROUP_DIM)
    tc = jnp.asarray(np.broadcast_to((np.cos(at) * scale).reshape(n, 1), (n, w)), f32)
    ts = jnp.asarray(np.broadcast_to((np.sin(at) * scale).reshape(n, 1), (n, w)), f32)
    const = lambda a: pl.BlockSpec(a.shape, lambda bi: (0,) * a.ndim)
    return pl.pallas_call(
        functools.partial(_fourier_kernel, r=r),
        grid=(b,),
        in_specs=[pl.BlockSpec((None, n, w), lambda bi: (bi, 0, 0)),
                  const(cs), const(d1), const(d3), const(tc), const(ts)],
        out_specs=pl.BlockSpec((None, n, w), lambda bi: (bi, 0, 0)),
        out_shape=jax.ShapeDtypeStruct((b, n, w), f32),
        scratch_shapes=[pltpu.VMEM((2 * w // LANES, n, LANES), f32)] + [pltpu.VMEM((w // LANES, n, LANES), f32)] * 3,
        compiler_params=_cparams(("parallel",), 48 * 1024 * 1024),
        name="fourier_mix",
    )(f, cs, d1, d3, tc, ts)


def _fourier_direct_kernel(f_ref, cs_ref, dn_ref, o_ref):
    w = FOURIER_WIDTH
    g = jnp.dot(f_ref[...], cs_ref[...].astype(bf16), preferred_element_type=f32)
    rhs = jnp.concatenate([g[:, :w], g[:, w:]], axis=0).astype(bf16)
    o_ref[...] = jnp.dot(dn_ref[...].astype(bf16), rhs, preferred_element_type=f32)


def _fourier_mix_direct(f):
    b, n, w = f.shape
    cs = jnp.asarray(_dft_tables_channel(), f32)
    a = 2.0 * np.pi * (np.outer(np.arange(n), np.arange(n)) % n) / n
    scale = 1.0 / math.sqrt(n * FOURIER_GROUP_DIM)
    dn = jnp.asarray(np.concatenate([np.cos(a), np.sin(a)], axis=1) * scale, f32)
    const = lambda t: pl.BlockSpec(t.shape, lambda bi: (0,) * t.ndim)
    return pl.pallas_call(
        _fourier_direct_kernel, grid=(b,),
        in_specs=[pl.BlockSpec((None, n, w), lambda bi: (bi, 0, 0)), const(cs), const(dn)],
        out_specs=pl.BlockSpec((None, n, w), lambda bi: (bi, 0, 0)),
        out_shape=jax.ShapeDtypeStruct((b, n, w), f32),
        compiler_params=_cparams(("parallel",)), name="fourier_mix_direct",
    )(f, cs, dn)


def _to_rows(t):
    st = jnp.stack([t[:, c * LANES:(c + 1) * LANES] for c in range(ROW_TILES)], axis=0)
    return jnp.swapaxes(st, 0, 1)


def _from_rows(rows):
    t = jnp.swapaxes(rows, 0, 1)
    return [t[c] for c in range(ROW_TILES)]


def _merge_kernel(na_ref, bg_ref, z_ref, zp_ref, zn_ref, fo_ref, x_ref, cw_ref, gmix_ref, wout_ref,
                  gpost_ref, g1_ref, sh2_ref, sc2_ref, gpre_ref, wr_ref,
                  xo_ref, h2_ref, lg_ref):
    i = pl.program_id(1)
    nt = pl.num_programs(1)
    tm = z_ref.shape[0]
    z = z_ref[...]
    zp = jnp.where(i > 0, zp_ref[SUBLANES - 1:SUBLANES, :], 0.0)
    zn = jnp.where(i < nt - 1, zn_ref[0:1, :], 0.0)
    rid = lax.broadcasted_iota(i32, z.shape, 0)
    z_dn = jnp.where(rid == 0, zp, pltpu.roll(z, 1, axis=0))
    z_up = jnp.where(rid == tm - 1, zn, pltpu.roll(z, tm - 1, axis=0))
    cv = bg_ref[...] * (cw_ref[0:1, :] * z_dn + cw_ref[1:2, :] * z + cw_ref[2:3, :] * z_up)

    a, b_ = NA_WIDTH, NA_WIDTH + CONV_CH
    parts = ((na_ref[...], 0, a), (cv, a, b_), (fo_ref[...], b_, D_MODEL))
    m = None
    for t, lo, hi in parts:
        y = (_rms(t) * gmix_ref[:, lo:hi]).astype(bf16)
        d = jnp.dot(y, wout_ref[lo:hi, :], preferred_element_type=f32)
        m = d if m is None else m + d
    xn = x_ref[...] + g1_ref[...] * (_rms(m) * gpost_ref[...])
    xo_ref[...] = xn
    h2 = (_rms(xn) * gpre_ref[...]) * (1.0 + sc2_ref[...]) + sh2_ref[...]
    h2_ref[...] = _to_rows(h2)
    hh = h2.astype(bf16)
    hl = (h2 - hh.astype(f32)).astype(bf16)
    wr = wr_ref[...]
    wh = wr.astype(bf16)
    wl = (wr - wh.astype(f32)).astype(bf16)
    nt_dot = lambda u, v: lax.dot_general(u, v, _NT, preferred_element_type=f32)
    lg_ref[...] = nt_dot(wh, hh) + (nt_dot(wh, hl) + nt_dot(wl, hh))


def _merge(na, bg, z, fo, x, modl, mod_row, conv_w, g_mix, w_out_b, g_post, g_pre_ffn, w_router_t, tm):
    b, n, d = x.shape
    hb = tm // SUBLANES
    nhb = n // SUBLANES
    row = lambda c: pl.BlockSpec((None, None, 1, d), lambda bi, i: (mod_row(bi), c, 0, 0))
    tok = lambda w: pl.BlockSpec((None, tm, w), lambda bi, i: (bi, i, 0))
    const = lambda a: pl.BlockSpec(a.shape, lambda bi, i: (0,) * a.ndim)
    vec = lambda a: a.reshape(1, -1)
    nt = n // tm
    args = (na, bg, z, z, z, fo, x, conv_w, vec(g_mix), w_out_b, vec(g_post), modl, modl, modl,
            vec(g_pre_ffn), w_router_t)
    return pl.pallas_call(
        _merge_kernel,
        grid=(b, nt),
        in_specs=[tok(NA_WIDTH), tok(CONV_CH), tok(CONV_CH),
                  pl.BlockSpec((None, SUBLANES, CONV_CH), lambda bi, i: (bi, jnp.maximum(i * hb - 1, 0), 0)),
                  pl.BlockSpec((None, SUBLANES, CONV_CH), lambda bi, i: (bi, jnp.minimum((i + 1) * hb, nhb - 1), 0)),
                  tok(FOURIER_WIDTH), tok(d), const(conv_w), const(vec(g_mix)), const(w_out_b),
                  const(vec(g_post)), row(2), row(3), row(4), const(vec(g_pre_ffn)), const(w_router_t)],
        out_specs=[tok(d),
                   pl.BlockSpec((tm, ROW_TILES, LANES), lambda bi, i: (bi * nt + i, 0, 0)),
                   pl.BlockSpec((None, N_EXPERTS, tm), lambda bi, i: (bi, 0, i))],
        out_shape=[jax.ShapeDtypeStruct((b, n, d), f32),
                   jax.ShapeDtypeStruct((b * n, ROW_TILES, LANES), f32),
                   jax.ShapeDtypeStruct((b, N_EXPERTS, n), f32)],
        compiler_params=_cparams(("parallel", "parallel"), 40 * 1024 * 1024),
        name="merge",
    )(*args)


def _split3(a):
    h = a.astype(bf16)
    r1 = a - h.astype(f32)
    m = r1.astype(bf16)
    l = (r1 - m.astype(f32)).astype(bf16)
    return h, m, l


def _threshold_search(aff, count, cap):
    t = jnp.zeros(aff.shape, i32)
    for bit in range(30, -1, -1):
        cand = t | (1 << bit)
        t = jnp.where(count(aff >= lax.bitcast_convert_type(cand, f32)) >= cap, cand, t)
    return lax.bitcast_convert_type(t, f32)


def _route_kernel(lg_ref, o_ref, *, nch, cap):
    ne = N_EXPERTS
    r = ne * nch
    sh = nch.bit_length() - 1
    lg = lg_ref[...]
    ex = jnp.exp(lg - jnp.max(lg, axis=0, keepdims=True))
    aff = (ex / jnp.sum(ex, axis=0, keepdims=True)).reshape(r, LANES)

    li = lax.broadcasted_iota(i32, (LANES, LANES), 0)
    lj = lax.broadcasted_iota(i32, (LANES, LANES), 1)
    triu = _ones_where(li <= lj)
    tril = _ones_where(lj <= li)
    eye = _ones_where(li == lj)
    ones = jnp.ones((LANES, LANES), bf16)
    ri = lax.broadcasted_iota(i32, (r, r), 0)
    rj = lax.broadcasted_iota(i32, (r, r), 1)
    same = (ri >> sh) == (rj >> sh)
    bd = _ones_where(same)
    sl = _ones_where(same & (rj < ri))
    mm = lambda u, v: jnp.dot(u, v, preferred_element_type=f32)

    def count(mask):
        return mm(bd, mm(_ones_where(mask), ones).astype(bf16))

    thr = _threshold_search(aff, count, cap)
    gt = aff > thr
    eq = aff == thr
    need = cap - count(gt)
    eqb = _ones_where(eq)
    pe = mm(eqb, triu) + mm(sl, mm(eqb, ones).astype(bf16))
    sel = gt | (eq & (pe <= need))
    selb = _ones_where(sel)
    rt = mm(selb, ones)
    oin = mm(sl, rt.astype(bf16)) + rt

    reps = cap // LANES
    tile = lambda a: jnp.concatenate([a] * reps, axis=1) if reps > 1 else a
    s_f = lax.broadcasted_iota(i32, (nch, cap), 1).astype(f32)
    s8 = lax.broadcasted_iota(i32, (SUBLANES, cap), 1).astype(f32)
    ji = lax.broadcasted_iota(i32, (nch, cap), 0).astype(f32)
    lis = lax.broadcasted_iota(i32, (LANES, cap), 0).astype(f32)
    ri8 = lax.broadcasted_iota(i32, (SUBLANES, cap), 0)
    ones8c = jnp.ones((SUBLANES, nch), bf16)
    ones8l = jnp.ones((SUBLANES, LANES), bf16)
    for e in range(ne):
        rows = slice(e * nch, (e + 1) * nch)
        zmask = tile(oin[rows]) <= s_f
        js = mm(ones8c, _ones_where(zmask))
        oex = mm(ones8c, jnp.where(zmask, tile(rt[rows]), 0.0).astype(bf16))
        s_loc = s8 - oex
        u = _ones_where(ji == jnp.broadcast_to(js[0:1], (nch, cap)))
        plt = lax.dot_general(tril, selb[rows], _NT, preferred_element_type=f32)
        ptg = mm(plt.astype(bf16), u)
        c = _ones_where(ptg <= jnp.broadcast_to(s_loc[0:1], (LANES, cap)))
        tl = mm(ones8l, c)
        idx = js * float(LANES) + tl
        affg = None
        for piece in _split3(aff[rows]):
            pt = lax.dot_general(eye, piece, _NT, preferred_element_type=f32).astype(bf16)
            pg = mm(pt, u)
            affg = pg if affg is None else affg + pg
        hit = lis == jnp.broadcast_to(tl[0:1], (LANES, cap))
        gate = jnp.sum(jnp.where(hit, affg, 0.0), axis=0, keepdims=True)
        o_ref[e] = jnp.where(ri8 == 0, idx, jnp.where(ri8 == 1, jnp.broadcast_to(gate, (SUBLANES, cap)), 0.0))


def _route_small_kernel(lg_ref, o_ref, *, cap):
    ne, n = lg_ref.shape
    lg = lg_ref[...]
    ex = jnp.exp(lg - jnp.max(lg, axis=0, keepdims=True))
    aff = ex / jnp.sum(ex, axis=0, keepdims=True)
    li = lax.broadcasted_iota(i32, (n, n), 0)
    lj = lax.broadcasted_iota(i32, (n, n), 1)
    triu = _ones_where(li <= lj)
    ones = jnp.ones((n, n), bf16)
    mm = lambda u, v: jnp.dot(u, v, preferred_element_type=f32)
    count = lambda mask: mm(_ones_where(mask), ones)
    thr = _threshold_search(aff, count, cap)
    gt = aff > thr
    eq = aff == thr
    need = cap - count(gt)
    sel = gt | (eq & (mm(_ones_where(eq), triu) <= need))
    slot = jnp.where(sel, mm(_ones_where(sel), triu) - 1.0, -1.0)
    tid = lax.broadcasted_iota(i32, (SUBLANES, n), 1).astype(f32)
    ri = lax.broadcasted_iota(i32, (SUBLANES, n), 0)
    s_i = lax.broadcasted_iota(i32, (cap, n), 0).astype(f32)
    ro = lax.broadcasted_iota(i32, (SUBLANES, cap), 0)
    bc = lambda a: jnp.broadcast_to(a, (SUBLANES, n))
    for e in range(ne):
        et = _ones_where(s_i == jnp.broadcast_to(slot[e:e + 1], (cap, n)))
        h, m, l = _split3(aff[e:e + 1])
        rm = jnp.where(ri == 0, tid, jnp.where(ri == 1, bc(h.astype(f32)), jnp.where(
            ri == 2, bc(m.astype(f32)), jnp.where(ri == 3, bc(l.astype(f32)), 0.0)))).astype(bf16)
        res = lax.dot_general(rm, et, _NT, preferred_element_type=f32)
        gate = res[1:2] + res[2:3] + res[3:4]
        o_ref[e] = jnp.where(ro == 0, jnp.broadcast_to(res[0:1], (SUBLANES, cap)),
                             jnp.where(ro == 1, jnp.broadcast_to(gate, (SUBLANES, cap)), 0.0))


def _route(logits_t):
    b, ne, n = logits_t.shape
    cap = CAPACITY_FACTOR * n // N_EXPERTS
    out_shape = jax.ShapeDtypeStruct((b, ne, SUBLANES, cap), f32)
    out_spec = pl.BlockSpec((None, ne, SUBLANES, cap), lambda bi: (bi, 0, 0, 0))
    nch = n // LANES
    if nch % SUBLANES == 0 and cap % LANES == 0:
        res = pl.pallas_call(
            functools.partial(_route_kernel, nch=nch, cap=cap), grid=(b,),
            in_specs=[pl.BlockSpec((None, ne, nch, LANES), lambda bi: (bi, 0, 0, 0))],
            out_specs=out_spec, out_shape=out_shape,
            compiler_params=_cparams(("parallel",), 40 * 1024 * 1024), name="route",
        )(logits_t.reshape(b, ne, nch, LANES))
    else:
        res = pl.pallas_call(
            functools.partial(_route_small_kernel, cap=cap), grid=(b,),
            in_specs=[pl.BlockSpec((None, ne, n), lambda bi: (bi, 0, 0))],
            out_specs=out_spec, out_shape=out_shape,
            compiler_params=_cparams(("parallel",)), name="route_small",
        )(logits_t)
    idx = res[:, :, 0, :].astype(i32).reshape(b * ne, 1, cap)
    gates = res[:, :, 1, :].reshape(b * ne, 1, cap)
    return idx, gates


def _dispatch_kernel(idx_ref, h_ref, o_ref, scr):
    cap = o_ref.shape[0]

    def body(i, c):
        scr[i] = h_ref[idx_ref[0, 0, i]]
        return c

    lax.fori_loop(0, cap, body, 0, unroll=8)
    tiles = _from_rows(scr[...])
    for c in range(ROW_TILES):
        o_ref[:, c * LANES:(c + 1) * LANES] = tiles[c].astype(bf16)


def _dispatch(h2_rows, idx, b):
    n = h2_rows.shape[0] // b
    cap = idx.shape[2]
    ne = N_EXPERTS
    return pl.pallas_call(
        _dispatch_kernel,
        grid=(b, ne),
        in_specs=[pl.BlockSpec((1, 1, cap), lambda bi, e: (bi * ne + e, 0, 0), memory_space=pltpu.SMEM),
                  pl.BlockSpec((n, ROW_TILES, LANES), lambda bi, e: (bi, 0, 0))],
        out_specs=pl.BlockSpec((None, cap, D_MODEL), lambda bi, e: (e, bi, 0)),
        out_shape=jax.ShapeDtypeStruct((ne, b * cap, D_MODEL), bf16),
        scratch_shapes=[pltpu.VMEM((cap, ROW_TILES, LANES), f32)],
        compiler_params=_cparams(("parallel", "arbitrary"), 48 * 1024 * 1024),
        name="dispatch",
    )(idx, h2_rows)


def _combine_kernel(idx_ref, g_ref, y_ref, x_ref, g2_ref, gp_ref, o_ref, acc, ys, *, chunk):
    s = pl.program_id(1)
    ne = N_EXPERTS
    cap = y_ref.shape[0]

    @pl.when(s == 0)
    def _():
        acc[...] = jnp.zeros_like(acc)

    @pl.when(s < ne)
    def _():
        ys[...] = _to_rows(y_ref[...].astype(f32))

        def body(i, c):
            t = idx_ref[0, 0, i]
            acc[t] = acc[t] + g_ref[0, 0, i] * ys[i]
            return c

        lax.fori_loop(0, cap, body, 0, unroll=8)

    @pl.when(s >= ne)
    def _():
        start = pl.multiple_of((s - ne) * chunk, chunk)
        tiles = _from_rows(acc[pl.ds(start, chunk)])
        ss = None
        for t in tiles:
            q = jnp.sum(t * t, axis=-1, keepdims=True)
            ss = q if ss is None else ss + q
        inv = lax.rsqrt(ss * (1.0 / D_MODEL) + RMS_EPS)
        for c in range(ROW_TILES):
            cs = slice(c * LANES, (c + 1) * LANES)
            o_ref[:, cs] = x_ref[:, cs] + g2_ref[:, cs] * ((tiles[c] * inv) * gp_ref[:, cs])


def _combine(y, idx, gates, x, modl, mod_row, g_post):
    b, n, d = x.shape
    ne = N_EXPERTS
    cap = idx.shape[2]
    chunk = min(n, 512)
    nchunk = n // chunk
    last = ne - 1
    sm = lambda: pl.BlockSpec((1, 1, cap), lambda bi, s: (bi * ne + jnp.minimum(s, last), 0, 0),
                              memory_space=pltpu.SMEM)
    return pl.pallas_call(
        functools.partial(_combine_kernel, chunk=chunk),
        grid=(b, ne + nchunk),
        in_specs=[sm(), sm(),
                  pl.BlockSpec((None, cap, D_MODEL), lambda bi, s: (jnp.minimum(s, last), bi, 0)),
                  pl.BlockSpec((None, chunk, d), lambda bi, s: (bi, jnp.maximum(s - ne, 0), 0)),
                  pl.BlockSpec((None, None, 1, d), lambda bi, s: (mod_row(bi), N_MOD - 1, 0, 0)),
                  pl.BlockSpec((1, d), lambda bi, s: (0, 0))],
        out_specs=pl.BlockSpec((None, chunk, D_MODEL), lambda bi, s: (bi, jnp.maximum(s - ne, 0), 0)),
        out_shape=jax.ShapeDtypeStruct((b, n, D_MODEL), f32),
        scratch_shapes=[pltpu.VMEM((n, ROW_TILES, LANES), f32), pltpu.VMEM((cap, ROW_TILES, LANES), f32)],
        compiler_params=_cparams(("parallel", "arbitrary"), 48 * 1024 * 1024),
        name="combine",
    )(idx, gates, y, x, modl, g_post.reshape(1, d))


def _ffn_kernel(*refs, nseg, nf):
    xs = refs[:nseg]
    wg_ref, wu_ref, wd_ref = refs[nseg:nseg + 3]
    ys = refs[nseg + 3:2 * nseg + 3]
    hs = refs[2 * nseg + 3:]
    s = pl.program_id(1)

    @pl.when(s < nf)
    def _():
        wg = wg_ref[...].astype(bf16)
        wu = wu_ref[...].astype(bf16)
        for x_ref, h_scr in zip(xs, hs):
            x = x_ref[...]
            g = jnp.dot(x, wg, preferred_element_type=f32)
            u = jnp.dot(x, wu, preferred_element_type=f32)
            h_scr[s] = ((g * jax.nn.sigmoid(g)) * u).astype(bf16)

    @pl.when(s >= nf)
    def _():
        wd = wd_ref[...].astype(bf16)
        for y_ref, h_scr in zip(ys, hs):
            h = jnp.concatenate([h_scr[f] for f in range(nf)], axis=1)
            y_ref[...] = jnp.dot(h, wd, preferred_element_type=f32)


def _expert_ffn(xs_list, w_gate, w_up, w_down, layer):
    _, ne, d, fh = w_gate.shape
    tf = 256
    tn = 256
    nf = fh // tf
    nd = d // tn
    nseg = len(xs_list)
    xspec = lambda m: pl.BlockSpec((None, m, d), lambda e, s: (e, 0, 0))
    wspec = pl.BlockSpec((None, None, d, tf), lambda e, s: (layer, e, 0, jnp.minimum(s, nf - 1)))
    dspec = pl.BlockSpec((None, None, fh, tn), lambda e, s: (layer, e, 0, jnp.maximum(s - nf, 0)))
    yspec = lambda m: pl.BlockSpec((None, m, tn), lambda e, s: (e, 0, jnp.maximum(s - nf, 0)))
    ms = [x.shape[1] for x in xs_list]
    return pl.pallas_call(
        functools.partial(_ffn_kernel, nseg=nseg, nf=nf),
        grid=(ne, nf + nd),
        in_specs=[xspec(m) for m in ms] + [wspec, wspec, dspec],
        out_specs=[yspec(m) for m in ms],
        out_shape=[jax.ShapeDtypeStruct((ne, m, d), f32) for m in ms],
        scratch_shapes=[pltpu.VMEM((nf, m, tf), bf16) for m in ms],
        compiler_params=_cparams(("parallel", "arbitrary"), VMEM_LIMIT),
        name="expert_ffn",
    )(*xs_list, w_gate, w_up, w_down)


def kernel(x, c, ctx, c_ctx, w_mod, b_mod, g_pre_mix, g_post_mix, g_pre_ffn, g_post_ffn, w_in, rpb,
           conv_w, g_mix_out, w_out, w_router, w_gate, w_up, w_down):
    depth = w_mod.shape[0]
    b, n, d = x.shape
    l_ctx = ctx.shape[1]
    ctx_row = b
    cvec = jnp.concatenate([c, c_ctx[None, :], jnp.zeros((SUBLANES - b - 1, d), f32)], axis=0)
    mod = _modulation(cvec, w_mod, b_mod)
    lat_row = lambda bi: bi
    cx_row = lambda bi: ctx_row
    tm = min(n, 512)
    for layer in range(depth):
        last = layer == depth - 1
        modl = mod[layer].reshape(SUBLANES, N_MOD, 1, d)
        w_in_b = w_in[layer].astype(bf16)
        w_out_b = w_out[layer].astype(bf16)
        w_router_t = w_router[layer].T
        q, k, _, vt, bg, z, f = _in_projection(x, modl, lat_row, g_pre_mix[layer], w_in_b, tm)
        qc, kc, vc, vct, bgc, zc, fc = _in_projection(ctx, modl, cx_row, g_pre_mix[layer], w_in_b, l_ctx)
        na = _neighbourhood_attention(q, k, vt, kc, vct, _na_bias_table(rpb[layer]))
        fo = _fourier_mix(f)
        x, h2_rows, logits = _merge(na, bg, z, fo, x, modl, lat_row, conv_w[layer], g_mix_out[layer],
                                    w_out_b, g_post_mix[layer], g_pre_ffn[layer], w_router_t, tm)
        idx, gates = _route(logits)
        xs_list = [_dispatch(h2_rows, idx, b)]
        if not last:
            na_c = _context_attention(qc, kc, vc)
            fo_c = _fourier_mix_direct(fc)
            ctx, h2c_rows, logits_c = _merge(na_c, bgc, zc, fo_c, ctx, modl, cx_row, conv_w[layer],
                                             g_mix_out[layer], w_out_b, g_post_mix[layer],
                                             g_pre_ffn[layer], w_router_t, l_ctx)
            idx_c, gates_c = _route(logits_c)
            xs_list.append(_dispatch(h2c_rows, idx_c, b))
        ys = _expert_ffn(xs_list, w_gate, w_up, w_down, layer)
        x = _combine(ys[0], idx, gates, x, modl, lat_row, g_post_ffn[layer])
        if not last:
            ctx = _combine(ys[1], idx_c, gates_c, ctx, modl, cx_row, g_post_ffn[layer])
    return x
```

```python
import functools
import math

import numpy as np
import jax
import jax.numpy as jnp
from jax import lax
from jax.experimental import pallas as pl
from jax.experimental.pallas import tpu as pltpu

f32 = jnp.float32
bf16 = jnp.bfloat16
i32 = jnp.int32

D_MODEL = 1024
GRID_W = 64
HEAD_DIM = 64
NA_WIDTH = 512
NA_HEADS = 8
NA_ROWS = 8
NA_COLS = 16
CONV_CH = 256
FOURIER_WIDTH = 256
FOURIER_GROUP_DIM = 64
N_EXPERTS = 16
CAPACITY_FACTOR = 2
N_MOD = 6
RMS_EPS = 1e-6
NEG_INF = -1e30

LANES = 128
SUBLANES = 8
ROW_TILES = D_MODEL // LANES
VMEM_LIMIT = 56 * 1024 * 1024

_NT = (((1,), (1,)), ((), ()))


def _cparams(sem, vmem=None):
    return pltpu.CompilerParams(dimension_semantics=sem, vmem_limit_bytes=vmem)


def _rms(x):
    return x * lax.rsqrt(jnp.mean(x * x, axis=-1, keepdims=True) + RMS_EPS)


def _ones_where(mask, dtype=bf16):
    return jnp.where(mask, 1.0, 0.0).astype(dtype)


def _mod_kernel(c_ref, w_ref, b_ref, o_ref):
    c = c_ref[...]
    s = c * jax.nn.sigmoid(c)
    o_ref[...] = jnp.dot(s, w_ref[...], preferred_element_type=f32,
                         precision=lax.Precision.HIGHEST) + b_ref[...]


def _modulation(cvec, w_mod, b_mod):
    depth, d, n6 = w_mod.shape
    tn = 1536
    return pl.pallas_call(
        _mod_kernel,
        grid=(depth, n6 // tn),
        in_specs=[
            pl.BlockSpec((SUBLANES, d), lambda l, j: (0, 0)),
            pl.BlockSpec((None, d, tn), lambda l, j: (l, 0, j)),
            pl.BlockSpec((None, 1, tn), lambda l, j: (l, 0, j)),
        ],
        out_specs=pl.BlockSpec((None, SUBLANES, tn), lambda l, j: (l, 0, j)),
        out_shape=jax.ShapeDtypeStruct((depth, SUBLANES, n6), f32),
        compiler_params=_cparams(("parallel", "parallel"), 40 * 1024 * 1024),
        name="modulation",
    )(cvec, w_mod, b_mod.reshape(depth, 1, n6))


def _inproj_kernel(x_ref, sh_ref, sc_ref, g_ref, w_ref, wvt_ref, q_ref, k_ref, v_ref, vt_ref, bg_ref, z_ref, f_ref):
    x = x_ref[...]
    h = (_rms(x) * g_ref[...]) * (1.0 + sc_ref[...]) + sh_ref[...]
    hb = h.astype(bf16)

    def proj(a, b):
        return jnp.dot(hb, w_ref[:, a:b], preferred_element_type=f32)

    o = 3 * NA_WIDTH
    q_ref[...] = (proj(0, NA_WIDTH) * (HEAD_DIM ** -0.5)).astype(bf16)
    k_ref[...] = proj(NA_WIDTH, 2 * NA_WIDTH).astype(bf16)
    v_ref[...] = proj(2 * NA_WIDTH, o).astype(bf16)
    vt_ref[...] = lax.dot_general(wvt_ref[...], hb, _NT, preferred_element_type=f32).astype(bf16)
    bg_ref[...] = proj(o, o + CONV_CH)
    z_ref[...] = proj(o + CONV_CH, o + 2 * CONV_CH) * proj(o + 2 * CONV_CH, o + 3 * CONV_CH)
    f_ref[...] = proj(o + 3 * CONV_CH, o + 3 * CONV_CH + FOURIER_WIDTH).astype(bf16)


def _in_projection(x, modl, mod_row, g_pre, w_in_b, tm):
    b, n, d = x.shape
    in_w = w_in_b.shape[1]
    wvt = w_in_b[:, 2 * NA_WIDTH:3 * NA_WIDTH].T
    row = lambda c: pl.BlockSpec((None, None, 1, d), lambda bi, i: (mod_row(bi), c, 0, 0))
    tok = lambda w: pl.BlockSpec((None, tm, w), lambda bi, i: (bi, i, 0))
    shp = lambda w, dt: jax.ShapeDtypeStruct((b, n, w), dt)
    return pl.pallas_call(
        _inproj_kernel,
        grid=(b, n // tm),
        in_specs=[tok(d), row(0), row(1),
                  pl.BlockSpec((1, d), lambda bi, i: (0, 0)),
                  pl.BlockSpec((d, in_w), lambda bi, i: (0, 0)),
                  pl.BlockSpec((NA_WIDTH, d), lambda bi, i: (0, 0))],
        out_specs=[tok(NA_WIDTH), tok(NA_WIDTH), tok(NA_WIDTH),
                   pl.BlockSpec((None, NA_WIDTH, tm), lambda bi, i: (bi, 0, i)),
                   tok(CONV_CH), tok(CONV_CH), tok(FOURIER_WIDTH)],
        out_shape=[shp(NA_WIDTH, bf16), shp(NA_WIDTH, bf16), shp(NA_WIDTH, bf16),
                   jax.ShapeDtypeStruct((b, NA_WIDTH, n), bf16),
                   shp(CONV_CH, f32), shp(CONV_CH, f32), shp(FOURIER_WIDTH, bf16)],
        compiler_params=_cparams(("parallel", "parallel"), 40 * 1024 * 1024),
        name="in_projection",
    )(x, modl, modl, g_pre.reshape(1, d), w_in_b, wvt)


def _softmax_pv(parts):
    m = None
    for s, _ in parts:
        mi = jnp.max(s, axis=-1, keepdims=True)
        m = mi if m is None else jnp.maximum(m, mi)
    l = None
    o = None
    for s, v in parts:
        e = jnp.exp(s - m)
        li = jnp.sum(e, axis=-1, keepdims=True)
        oi = jnp.dot(e.astype(bf16), v, preferred_element_type=f32)
        l = li if l is None else l + li
        o = oi if o is None else o + oi
    return o * (1.0 / l)


def _na_kernel(q_ref, k_ref, vt_ref, kc_ref, vct_ref, bias_ref, o_ref, w_scr, *, rows, rb):
    i = pl.program_id(1)
    win = NA_ROWS * GRID_W
    npair = NA_HEADS // 2
    nwin = rows - 1
    half = NA_ROWS // 2

    @pl.when(i == 0)
    def _():
        for p in range(npair):
            vt = vt_ref[p * LANES:(p + 1) * LANES, :]
            for j in range(nwin):
                w_scr[p * nwin + j] = vt[:, j * GRID_W:j * GRID_W + LANES]

    lo = lax.broadcasted_iota(i32, (GRID_W, LANES), 1) < HEAD_DIM

    def row_body(j, carry):
        r = i * rb + j
        rs = jnp.clip(r - half, 0, rows - NA_ROWS)
        dr0 = (NA_ROWS - 1) - (r - rs)
        kstart = pl.multiple_of(rs * GRID_W, GRID_W)
        qstart = pl.multiple_of(j * GRID_W, GRID_W)
        for p in range(npair):
            ls = slice(p * LANES, (p + 1) * LANES)
            qp = q_ref[pl.ds(qstart, GRID_W), ls]
            zero = jnp.zeros_like(qp)
            q2 = jnp.concatenate([jnp.where(lo, qp, zero), jnp.where(lo, zero, qp)], axis=0)
            s_w = lax.dot_general(k_ref[pl.ds(kstart, win), ls], q2, _NT, preferred_element_type=f32)
            s_c = lax.dot_general(kc_ref[:, ls], q2, _NT, preferred_element_type=f32)
            s_w = s_w + jnp.concatenate([bias_ref[p, dr0 + 2 * jj] for jj in range(half)], axis=0)
            m = jnp.maximum(jnp.max(s_w, axis=0, keepdims=True), jnp.max(s_c, axis=0, keepdims=True))
            e_w = jnp.exp(s_w - m)
            e_c = jnp.exp(s_c - m)
            l = jnp.sum(e_w, axis=0, keepdims=True) + jnp.sum(e_c, axis=0, keepdims=True)
            vtw = jnp.concatenate([w_scr[p * nwin + rs + 2 * jj] for jj in range(half)], axis=1)
            ot = (jnp.dot(vtw, e_w.astype(bf16), preferred_element_type=f32)
                  + jnp.dot(vct_ref[ls, :], e_c.astype(bf16), preferred_element_type=f32))
            of = (ot * (1.0 / l)).T
            o_ref[pl.ds(qstart, GRID_W), ls] = jnp.where(lo, of[:GRID_W], of[GRID_W:])
        return carry

    lax.fori_loop(0, rb, row_body, 0)


def _bias_pairs_kernel(rv_ref, oh_ref, mask_ref, o_ref):
    acc = None
    for piece in _split3(rv_ref[...]):
        d = jnp.dot(piece, oh_ref[...], preferred_element_type=f32)
        acc = d if acc is None else acc + d
    o_ref[...] = acc + mask_ref[...]


def _na_bias_table(rpb_l):
    nh, ndr, ndc = rpb_l.shape
    pad = 32
    cols = np.arange(GRID_W)
    dc = np.clip(cols[None, :] - cols[:, None], -(NA_COLS - 1), NA_COLS - 1) + (NA_COLS - 1)
    col_start = np.clip(cols - NA_COLS // 2, 0, GRID_W - NA_COLS)
    col_in = (cols[None, :] >= col_start[:, None]) & (cols[None, :] < col_start[:, None] + NA_COLS)
    oh = np.zeros((2, 2, pad, 2, GRID_W, 2, GRID_W), np.float32)
    for par in range(2):
        for s in range(2):
            oh[par, s, dc, s, cols[None, :], par, cols[:, None]] = 1.0
    ncol = 2 * GRID_W * LANES
    oh = jnp.asarray(oh.reshape(4 * pad, ncol), bf16)
    mask = np.where(np.broadcast_to(col_in.T[None, :, None, :], (2, GRID_W, 2, GRID_W)), 0.0, NEG_INF)
    mask = jnp.asarray(mask.reshape(1, ncol), f32)
    rp = jnp.pad(rpb_l, ((0, 0), (0, 0), (0, pad - ndc)))
    rv = jnp.concatenate([rp[:, 0:ndr - 1], rp[:, 1:ndr]], axis=-1)
    rv = rv.reshape(nh // 2, 2, ndr - 1, 2 * pad).transpose(0, 2, 1, 3).reshape(nh // 2 * (ndr - 1), 4 * pad)
    tn = 2048
    out = pl.pallas_call(
        _bias_pairs_kernel, grid=(ncol // tn,),
        in_specs=[pl.BlockSpec(rv.shape, lambda j: (0, 0)),
                  pl.BlockSpec((4 * pad, tn), lambda j: (0, j)),
                  pl.BlockSpec((1, tn), lambda j: (0, j))],
        out_specs=pl.BlockSpec((rv.shape[0], tn), lambda j: (0, j)),
        out_shape=jax.ShapeDtypeStruct((rv.shape[0], ncol), f32),
        compiler_params=_cparams(("parallel",)), name="na_bias_table",
    )(rv, oh, mask)
    return out.reshape(nh // 2, ndr - 1, LANES, LANES)


def _neighbourhood_attention(q, k, vt, kc, vct, bias):
    b, n, w = q.shape
    l = kc.shape[1]
    rows = n // GRID_W
    rb = 8
    return pl.pallas_call(
        functools.partial(_na_kernel, rows=rows, rb=rb),
        grid=(b, rows // rb),
        in_specs=[pl.BlockSpec((None, rb * GRID_W, w), lambda bi, i: (bi, i, 0)),
                  pl.BlockSpec((None, n, w), lambda bi, i: (bi, 0, 0)),
                  pl.BlockSpec((None, w, n), lambda bi, i: (bi, 0, 0)),
                  pl.BlockSpec((None, l, w), lambda bi, i: (bi, 0, 0)),
                  pl.BlockSpec((None, w, l), lambda bi, i: (bi, 0, 0)),
                  pl.BlockSpec(bias.shape, lambda bi, i: (0, 0, 0, 0))],
        out_specs=pl.BlockSpec((None, rb * GRID_W, w), lambda bi, i: (bi, i, 0)),
        out_shape=jax.ShapeDtypeStruct((b, n, w), f32),
        scratch_shapes=[pltpu.VMEM((NA_HEADS // 2 * (rows - 1), LANES, LANES), bf16)],
        compiler_params=_cparams(("parallel", "arbitrary"), 48 * 1024 * 1024),
        name="neighbourhood_attention",
    )(q, k, vt, kc, vct, bias)


def _ctx_attn_kernel(q_ref, k_ref, v_ref, o_ref):
    m = q_ref.shape[0]
    lo = lax.broadcasted_iota(i32, (m, LANES), 1) < HEAD_DIM
    for p in range(NA_HEADS // 2):
        ls = slice(p * LANES, (p + 1) * LANES)
        qp, kp, vp = q_ref[:, ls], k_ref[:, ls], v_ref[:, ls]
        outs = []
        for par in range(2):
            qm = jnp.where(lo if par == 0 else jnp.logical_not(lo), qp, jnp.zeros_like(qp))
            s = lax.dot_general(qm, kp, _NT, preferred_element_type=f32)
            outs.append(_softmax_pv([(s, vp)]))
        o_ref[:, ls] = jnp.where(lo, outs[0], outs[1])


def _context_attention(q, k, v):
    b, l, w = q.shape
    spec = pl.BlockSpec((None, l, w), lambda bi: (bi, 0, 0))
    return pl.pallas_call(
        _ctx_attn_kernel, grid=(b,), in_specs=[spec, spec, spec], out_specs=spec,
        out_shape=jax.ShapeDtypeStruct((b, l, w), f32),
        compiler_params=_cparams(("parallel",)), name="context_attention",
    )(q, k, v)


def _dft_tables_channel():
    c = np.arange(FOURIER_GROUP_DIM)
    ang = 2.0 * np.pi * np.outer(c, c) / FOURIER_GROUP_DIM
    g = FOURIER_WIDTH // FOURIER_GROUP_DIM
    cb = np.kron(np.eye(g), np.cos(ang))
    sb = np.kron(np.eye(g), np.sin(ang))
    return np.concatenate([cb, -sb], axis=1)


def _fourier_kernel(f_ref, cs_ref, d1_ref, d3_ref, tc_ref, ts_ref, o_ref, g_scr, br_scr, bi_scr, o_scr, *, r):
    nt = FOURIER_WIDTH // LANES
    g = jnp.dot(f_ref[...], cs_ref[...].astype(bf16), preferred_element_type=f32)
    for c in range(2 * nt):
        g_scr[c] = g[:, c * LANES:(c + 1) * LANES]
    d1 = d1_ref[...].astype(bf16)

    def strided(scr, tiles, start, size, stride):
        return jnp.concatenate([scr[c, pl.ds(start, size, stride=stride), :] for c in tiles], axis=1)

    for n2 in range(GRID_W):
        gr = strided(g_scr, range(nt), n2, r, GRID_W)
        gi = strided(g_scr, range(nt, 2 * nt), n2, r, GRID_W)
        rhs = jnp.concatenate([gr, gi], axis=0).astype(bf16)
        a = jnp.dot(d1, rhs, preferred_element_type=f32)
        ar, ai = a[:r], a[r:]
        tc = tc_ref[n2 * r:(n2 + 1) * r, :]
        ts = ts_ref[n2 * r:(n2 + 1) * r, :]
        br = ar * tc + ai * ts
        bi = ai * tc - ar * ts
        for c in range(nt):
            br_scr[c, n2 * r:(n2 + 1) * r, :] = br[:, c * LANES:(c + 1) * LANES]
            bi_scr[c, n2 * r:(n2 + 1) * r, :] = bi[:, c * LANES:(c + 1) * LANES]
    d3 = d3_ref[...].astype(bf16)
    for k1 in range(r):
        br = strided(br_scr, range(nt), k1, GRID_W, r)
        bi = strided(bi_scr, range(nt), k1, GRID_W, r)
        rhs = jnp.concatenate([br, bi], axis=0).astype(bf16)
        y = jnp.dot(d3, rhs, preferred_element_type=f32)
        for c in range(nt):
            o_scr[c, pl.ds(k1, GRID_W, stride=r), :] = y[:, c * LANES:(c + 1) * LANES]
    for c in range(nt):
        o_ref[:, c * LANES:(c + 1) * LANES] = o_scr[c]


def _fourier_mix(f):
    b, n, w = f.shape
    r = n // GRID_W
    cs = jnp.asarray(_dft_tables_channel(), f32)
    a1 = 2.0 * np.pi * np.outer(np.arange(r), np.arange(r)) / r
    c1, s1 = np.cos(a1), np.sin(a1)
    d1 = jnp.asarray(np.block([[c1, s1], [-s1, c1]]), f32)
    a3 = 2.0 * np.pi * np.outer(np.arange(GRID_W), np.arange(GRID_W)) / GRID_W
    d3 = jnp.asarray(np.concatenate([np.cos(a3), np.sin(a3)], axis=1), f32)
    n2 = np.arange(GRID_W)[:, None]
    k1 = np.arange(r)[None, :]
    at = 2.0 * np.pi * ((n2 * k1) % n) / n
    scale = 1.0 / math.sqrt(n * FOURIER_GROUP_DIM)
    tc = jnp.asarray(np.broadcast_to((np.cos(at) * scale).reshape(n, 1), (n, w)), f32)
    ts = jnp.asarray(np.broadcast_to((np.sin(at) * scale).reshape(n, 1), (n, w)), f32)
    const = lambda a: pl.BlockSpec(a.shape, lambda bi: (0,) * a.ndim)
    return pl.pallas_call(
        functools.partial(_fourier_kernel, r=r),
        grid=(b,),
        in_specs=[pl.BlockSpec((None, n, w), lambda bi: (bi, 0, 0)),
                  const(cs), const(d1), const(d3), const(tc), const(ts)],
        out_specs=pl.BlockSpec((None, n, w), lambda bi: (bi, 0, 0)),
        out_shape=jax.ShapeDtypeStruct((b, n, w), f32),
        scratch_shapes=[pltpu.VMEM((2 * w // LANES, n, LANES), f32)] + [pltpu.VMEM((w // LANES, n, LANES), f32)] * 3,
        compiler_params=_cparams(("parallel",), 48 * 1024 * 1024),
        name="fourier_mix",
    )(f, cs, d1, d3, tc, ts)


def _fourier_direct_kernel(f_ref, cs_ref, dn_ref, o_ref):
    w = FOURIER_WIDTH
    g = jnp.dot(f_ref[...], cs_ref[...].astype(bf16), preferred_element_type=f32)
    rhs = jnp.concatenate([g[:, :w], g[:, w:]], axis=0).astype(bf16)
    o_ref[...] = jnp.dot(dn_ref[...].astype(bf16), rhs, preferred_element_type=f32)


def _fourier_mix_direct(f):
    b, n, w = f.shape
    cs = jnp.asarray(_dft_tables_channel(), f32)
    a = 2.0 * np.pi * (np.outer(np.arange(n), np.arange(n)) % n) / n
    scale = 1.0 / math.sqrt(n * FOURIER_GROUP_DIM)
    dn = jnp.asarray(np.concatenate([np.cos(a), np.sin(a)], axis=1) * scale, f32)
    const = lambda t: pl.BlockSpec(t.shape, lambda bi: (0,) * t.ndim)
    return pl.pallas_call(
        _fourier_direct_kernel, grid=(b,),
        in_specs=[pl.BlockSpec((None, n, w), lambda bi: (bi, 0, 0)), const(cs), const(dn)],
        out_specs=pl.BlockSpec((None, n, w), lambda bi: (bi, 0, 0)),
        out_shape=jax.ShapeDtypeStruct((b, n, w), f32),
        compiler_params=_cparams(("parallel",)), name="fourier_mix_direct",
    )(f, cs, dn)


def _to_rows(t):
    st = jnp.stack([t[:, c * LANES:(c + 1) * LANES] for c in range(ROW_TILES)], axis=0)
    return jnp.swapaxes(st, 0, 1)


def _from_rows(rows):
    t = jnp.swapaxes(rows, 0, 1)
    return [t[c] for c in range(ROW_TILES)]


def _merge_kernel(na_ref, bg_ref, z_ref, zp_ref, zn_ref, fo_ref, x_ref, cw_ref, gmix_ref, wout_ref,
                  gpost_ref, g1_ref, sh2_ref, sc2_ref, gpre_ref, wr_ref,
                  xo_ref, h2_ref, lg_ref):
    i = pl.program_id(1)
    nt = pl.num_programs(1)
    tm = z_ref.shape[0]
    z = z_ref[...]
    zp = jnp.where(i > 0, zp_ref[SUBLANES - 1:SUBLANES, :], 0.0)
    zn = jnp.where(i < nt - 1, zn_ref[0:1, :], 0.0)
    rid = lax.broadcasted_iota(i32, z.shape, 0)
    z_dn = jnp.where(rid == 0, zp, pltpu.roll(z, 1, axis=0))
    z_up = jnp.where(rid == tm - 1, zn, pltpu.roll(z, tm - 1, axis=0))
    cv = bg_ref[...] * (cw_ref[0:1, :] * z_dn + cw_ref[1:2, :] * z + cw_ref[2:3, :] * z_up)

    a, b_ = NA_WIDTH, NA_WIDTH + CONV_CH
    parts = ((na_ref[...], 0, a), (cv, a, b_), (fo_ref[...], b_, D_MODEL))
    m = None
    for t, lo, hi in parts:
        y = (_rms(t) * gmix_ref[:, lo:hi]).astype(bf16)
        d = jnp.dot(y, wout_ref[lo:hi, :], preferred_element_type=f32)
        m = d if m is None else m + d
    xn = x_ref[...] + g1_ref[...] * (_rms(m) * gpost_ref[...])
    xo_ref[...] = xn
    h2 = (_rms(xn) * gpre_ref[...]) * (1.0 + sc2_ref[...]) + sh2_ref[...]
    h2_ref[...] = _to_rows(h2)
    hh = h2.astype(bf16)
    hl = (h2 - hh.astype(f32)).astype(bf16)
    wr = wr_ref[...]
    wh = wr.astype(bf16)
    wl = (wr - wh.astype(f32)).astype(bf16)
    nt_dot = lambda u, v: lax.dot_general(u, v, _NT, preferred_element_type=f32)
    lg_ref[...] = nt_dot(wh, hh) + (nt_dot(wh, hl) + nt_dot(wl, hh))


def _merge(na, bg, z, fo, x, modl, mod_row, conv_w, g_mix, w_out_b, g_post, g_pre_ffn, w_router_t, tm):
    b, n, d = x.shape
    hb = tm // SUBLANES
    nhb = n // SUBLANES
    row = lambda c: pl.BlockSpec((None, None, 1, d), lambda bi, i: (mod_row(bi), c, 0, 0))
    tok = lambda w: pl.BlockSpec((None, tm, w), lambda bi, i: (bi, i, 0))
    const = lambda a: pl.BlockSpec(a.shape, lambda bi, i: (0,) * a.ndim)
    vec = lambda a: a.reshape(1, -1)
    nt = n // tm
    args = (na, bg, z, z, z, fo, x, conv_w, vec(g_mix), w_out_b, vec(g_post), modl, modl, modl,
            vec(g_pre_ffn), w_router_t)
    return pl.pallas_call(
        _merge_kernel,
        grid=(b, nt),
        in_specs=[tok(NA_WIDTH), tok(CONV_CH), tok(CONV_CH),
                  pl.BlockSpec((None, SUBLANES, CONV_CH), lambda bi, i: (bi, jnp.maximum(i * hb - 1, 0), 0)),
                  pl.BlockSpec((None, SUBLANES, CONV_CH), lambda bi, i: (bi, jnp.minimum((i + 1) * hb, nhb - 1), 0)),
                  tok(FOURIER_WIDTH), tok(d), const(conv_w), const(vec(g_mix)), const(w_out_b),
                  const(vec(g_post)), row(2), row(3), row(4), const(vec(g_pre_ffn)), const(w_router_t)],
        out_specs=[tok(d),
                   pl.BlockSpec((tm, ROW_TILES, LANES), lambda bi, i: (bi * nt + i, 0, 0)),
                   pl.BlockSpec((None, N_EXPERTS, tm), lambda bi, i: (bi, 0, i))],
        out_shape=[jax.ShapeDtypeStruct((b, n, d), f32),
                   jax.ShapeDtypeStruct((b * n, ROW_TILES, LANES), f32),
                   jax.ShapeDtypeStruct((b, N_EXPERTS, n), f32)],
        compiler_params=_cparams(("parallel", "parallel"), 40 * 1024 * 1024),
        name="merge",
    )(*args)


def _split3(a):
    h = a.astype(bf16)
    r1 = a - h.astype(f32)
    m = r1.astype(bf16)
    l = (r1 - m.astype(f32)).astype(bf16)
    return h, m, l


def _threshold_search(aff, count, cap):
    t = jnp.zeros(aff.shape, i32)
    for lo_bit in range(27, -1, -3):
        digit = jnp.zeros(aff.shape, i32)
        for j in range(1, 8):
            cand = lax.bitcast_convert_type(t | (j << lo_bit), f32)
            digit = digit + jnp.where(count(aff >= cand) >= cap, 1, 0)
        t = t | (digit << lo_bit)
    return lax.bitcast_convert_type(t, f32)


def _route_kernel(lg_ref, o_ref, *, nch, cap):
    ne = N_EXPERTS
    r = ne * nch
    sh = nch.bit_length() - 1
    lg = lg_ref[...]
    ex = jnp.exp(lg - jnp.max(lg, axis=0, keepdims=True))
    aff = (ex / jnp.sum(ex, axis=0, keepdims=True)).reshape(r, LANES)

    li = lax.broadcasted_iota(i32, (LANES, LANES), 0)
    lj = lax.broadcasted_iota(i32, (LANES, LANES), 1)
    triu = _ones_where(li <= lj)
    tril = _ones_where(lj <= li)
    eye = _ones_where(li == lj)
    ones = jnp.ones((LANES, LANES), bf16)
    ri = lax.broadcasted_iota(i32, (r, r), 0)
    rj = lax.broadcasted_iota(i32, (r, r), 1)
    same = (ri >> sh) == (rj >> sh)
    sl =_ones_where(same & (rj < ri))
    mm = lambda u, v: jnp.dot(u, v, preferred_element_type=f32)

    vpe = nch // SUBLANES
    e8i = lax.broadcasted_iota(i32, (ne * SUBLANES, ne * SUBLANES), 0) >> 3
    e8j = lax.broadcasted_iota(i32, (ne * SUBLANES, ne * SUBLANES), 1) >> 3
    bd8 = _ones_where(e8i == e8j)

    def count(mask):
        m = jnp.where(mask, 1.0, 0.0).reshape(ne, vpe, SUBLANES, LANES)
        m = jnp.sum(m, axis=1).reshape(ne * SUBLANES, LANES).astype(bf16)
        tot = mm(mm(bd8, m).astype(bf16), ones)
        tot = jnp.broadcast_to(tot.reshape(ne, 1, SUBLANES, LANES), (ne, vpe, SUBLANES, LANES))
        return tot.reshape(r, LANES)

    thr = _threshold_search(aff, count, cap)
    gt = aff > thr
    eq = aff == thr
    need = cap - count(gt)
    eqb = _ones_where(eq)
    pe = mm(eqb, triu) + mm(sl, mm(eqb, ones).astype(bf16))
    sel = gt | (eq & (pe <= need))
    selb = _ones_where(sel)
    rt = mm(selb, ones)
    oin = mm(sl, rt.astype(bf16)) + rt

    reps = cap // LANES
    tile = lambda a: jnp.concatenate([a] * reps, axis=1) if reps > 1 else a
    s_f = lax.broadcasted_iota(i32, (nch, cap), 1).astype(f32)
    s8 = lax.broadcasted_iota(i32, (SUBLANES, cap), 1).astype(f32)
    ji = lax.broadcasted_iota(i32, (nch, cap), 0).astype(f32)
    lis = lax.broadcasted_iota(i32, (LANES, cap), 0).astype(f32)
    ri8 = lax.broadcasted_iota(i32, (SUBLANES, cap), 0)
    ones8c = jnp.ones((SUBLANES, nch), bf16)
    ones8l = jnp.ones((SUBLANES, LANES), bf16)
    for e in range(ne):
        rows = slice(e * nch, (e + 1) * nch)
        zmask = tile(oin[rows]) <= s_f
        js = mm(ones8c, _ones_where(zmask))
        oex = mm(ones8c, jnp.where(zmask, tile(rt[rows]), 0.0).astype(bf16))
        s_loc = s8 - oex
        u = _ones_where(ji == jnp.broadcast_to(js[0:1], (nch, cap)))
        plt = lax.dot_general(tril, selb[rows], _NT, preferred_element_type=f32)
        ptg = mm(plt.astype(bf16), u)
        c = _ones_where(ptg <= jnp.broadcast_to(s_loc[0:1], (LANES, cap)))
        tl = mm(ones8l, c)
        idx = js * float(LANES) + tl
        affg = None
        for piece in _split3(aff[rows]):
            pt = lax.dot_general(eye, piece, _NT, preferred_element_type=f32).astype(bf16)
            pg = mm(pt, u)
            affg = pg if affg is None else affg + pg
        hit = lis == jnp.broadcast_to(tl[0:1], (LANES, cap))
        gate = jnp.sum(jnp.where(hit, affg, 0.0), axis=0, keepdims=True)
        o_ref[e] = jnp.where(ri8 == 0, idx, jnp.where(ri8 == 1, jnp.broadcast_to(gate, (SUBLANES, cap)), 0.0))


def _route_small_kernel(lg_ref, o_ref, *, cap):
    ne, n = lg_ref.shape
    lg = lg_ref[...]
    ex = jnp.exp(lg - jnp.max(lg, axis=0, keepdims=True))
    aff = ex / jnp.sum(ex, axis=0, keepdims=True)
    li = lax.broadcasted_iota(i32, (n, n), 0)
    lj = lax.broadcasted_iota(i32, (n, n), 1)
    triu = _ones_where(li <= lj)
    ones = jnp.ones((n, n), bf16)
    mm = lambda u, v: jnp.dot(u, v, preferred_element_type=f32)
    count = lambda mask: mm(_ones_where(mask), ones)
    thr = _threshold_search(aff, count, cap)
    gt = aff > thr
    eq = aff == thr
    need = cap - count(gt)
    sel = gt | (eq & (mm(_ones_where(eq), triu) <= need))
    slot = jnp.where(sel, mm(_ones_where(sel), triu) - 1.0, -1.0)
    tid = lax.broadcasted_iota(i32, (SUBLANES, n), 1).astype(f32)
    ri = lax.broadcasted_iota(i32, (SUBLANES, n), 0)
    s_i = lax.broadcasted_iota(i32, (cap, n), 0).astype(f32)
    ro = lax.broadcasted_iota(i32, (SUBLANES, cap), 0)
    bc = lambda a: jnp.broadcast_to(a, (SUBLANES, n))
    for e in range(ne):
        et = _ones_where(s_i == jnp.broadcast_to(slot[e:e + 1], (cap, n)))
        h, m, l = _split3(aff[e:e + 1])
        rm = jnp.where(ri == 0, tid, jnp.where(ri == 1, bc(h.astype(f32)), jnp.where(
            ri == 2, bc(m.astype(f32)), jnp.where(ri == 3, bc(l.astype(f32)), 0.0)))).astype(bf16)
        res = lax.dot_general(rm, et, _NT, preferred_element_type=f32)
        gate = res[1:2] + res[2:3] + res[3:4]
        o_ref[e] = jnp.where(ro == 0, jnp.broadcast_to(res[0:1], (SUBLANES, cap)),
                             jnp.where(ro == 1, jnp.broadcast_to(gate, (SUBLANES, cap)), 0.0))


def _route(logits_t):
    b, ne, n = logits_t.shape
    cap = CAPACITY_FACTOR * n // N_EXPERTS
    out_shape = jax.ShapeDtypeStruct((b, ne, SUBLANES, cap), f32)
    out_spec = pl.BlockSpec((None, ne, SUBLANES, cap), lambda bi: (bi, 0, 0, 0))
    nch = n // LANES
    if nch % SUBLANES == 0 and cap % LANES == 0:
        res = pl.pallas_call(
            functools.partial(_route_kernel, nch=nch, cap=cap), grid=(b,),
            in_specs=[pl.BlockSpec((None, ne, nch, LANES), lambda bi: (bi, 0, 0, 0))],
            out_specs=out_spec, out_shape=out_shape,
            compiler_params=_cparams(("parallel",), 40 * 1024 * 1024), name="route",
        )(logits_t.reshape(b, ne, nch, LANES))
    else:
        res = pl.pallas_call(
            functools.partial(_route_small_kernel, cap=cap), grid=(b,),
            in_specs=[pl.BlockSpec((None, ne, n), lambda bi: (bi, 0, 0))],
            out_specs=out_spec, out_shape=out_shape,
            compiler_params=_cparams(("parallel",)), name="route_small",
        )(logits_t)
    idx = res[:, :, 0, :].astype(i32).reshape(b * ne, 1, cap)
    gates = res[:, :, 1, :].reshape(b * ne, 1, cap)
    return idx, gates


def _dispatch_kernel(idx_ref, h_ref, o_ref, scr):
    cap = o_ref.shape[0]

    def body(i, c):
        scr[i] = h_ref[idx_ref[0, 0, i]]
        return c

    lax.fori_loop(0, cap, body, 0, unroll=8)
    tiles = _from_rows(scr[...])
    for c in range(ROW_TILES):
        o_ref[:, c * LANES:(c + 1) * LANES] = tiles[c].astype(bf16)


def _dispatch(h2_rows, idx, b):
    n = h2_rows.shape[0] // b
    cap = idx.shape[2]
    ne = N_EXPERTS
    return pl.pallas_call(
        _dispatch_kernel,
        grid=(b, ne),
        in_specs=[pl.BlockSpec((1, 1, cap), lambda bi, e: (bi * ne + e, 0, 0), memory_space=pltpu.SMEM),
                  pl.BlockSpec((n, ROW_TILES, LANES), lambda bi, e: (bi, 0, 0))],
        out_specs=pl.BlockSpec((None, cap, D_MODEL), lambda bi, e: (e, bi, 0)),
        out_shape=jax.ShapeDtypeStruct((ne, b * cap, D_MODEL), bf16),
        scratch_shapes=[pltpu.VMEM((cap, ROW_TILES, LANES), f32)],
        compiler_params=_cparams(("parallel", "arbitrary"), 48 * 1024 * 1024),
        name="dispatch",
    )(idx, h2_rows)


def _combine_kernel(idx_ref, g_ref, y_ref, x_ref, g2_ref, gp_ref, o_ref, acc, ys, *, chunk):
    s = pl.program_id(1)
    ne = N_EXPERTS
    cap = y_ref.shape[0]

    @pl.when(s == 0)
    def _():
        acc[...] = jnp.zeros_like(acc)

    @pl.when(s < ne)
    def _():
        ys[...] = _to_rows(y_ref[...].astype(f32))

        group = 8

        def body(gi, c):
            base = gi * group
            ts = [idx_ref[0, 0, base + k] for k in range(group)]
            vals = [acc[ts[k]] + g_ref[0, 0, base + k] * ys[base + k] for k in range(group)]
            for k in range(group):
                acc[ts[k]] = vals[k]
            return c

        lax.fori_loop(0, cap // group, body, 0, unroll=2)

    @pl.when(s >= ne)
    def _():
        start = pl.multiple_of((s - ne) * chunk, chunk)
        tiles = _from_rows(acc[pl.ds(start, chunk)])
        ss = None
        for t in tiles:
            q = jnp.sum(t * t, axis=-1, keepdims=True)
            ss = q if ss is None else ss + q
        inv = lax.rsqrt(ss * (1.0 / D_MODEL) + RMS_EPS)
        for c in range(ROW_TILES):
            cs = slice(c * LANES, (c + 1) * LANES)
            o_ref[:, cs] = x_ref[:, cs] + g2_ref[:, cs] * ((tiles[c] * inv) * gp_ref[:, cs])


def _combine(y, idx, gates, x, modl, mod_row, g_post):
    b, n, d = x.shape
    ne = N_EXPERTS
    cap = idx.shape[2]
    chunk = min(n, 512)
    nchunk = n // chunk
    last = ne - 1
    sm = lambda: pl.BlockSpec((1, 1, cap), lambda bi, s: (bi * ne + jnp.minimum(s, last), 0, 0),
                              memory_space=pltpu.SMEM)
    return pl.pallas_call(
        functools.partial(_combine_kernel, chunk=chunk),
        grid=(b, ne + nchunk),
        in_specs=[sm(), sm(),
                  pl.BlockSpec((None, cap, D_MODEL), lambda bi, s: (jnp.minimum(s, last), bi, 0)),
                  pl.BlockSpec((None, chunk, d), lambda bi, s: (bi, jnp.maximum(s - ne, 0), 0)),
                  pl.BlockSpec((None, None, 1, d), lambda bi, s: (mod_row(bi), N_MOD - 1, 0, 0)),
                  pl.BlockSpec((1, d), lambda bi, s: (0, 0))],
        out_specs=pl.BlockSpec((None, chunk, D_MODEL), lambda bi, s: (bi, jnp.maximum(s - ne, 0), 0)),
        out_shape=jax.ShapeDtypeStruct((b, n, D_MODEL), f32),
        scratch_shapes=[pltpu.VMEM((n, ROW_TILES, LANES), f32), pltpu.VMEM((cap, ROW_TILES, LANES), f32)],
        compiler_params=_cparams(("parallel", "arbitrary"), 48 * 1024 * 1024),
        name="combine",
    )(idx, gates, y, x, modl, g_post.reshape(1, d))


def _ffn_kernel(*refs, nseg, nf):
    xs = refs[:nseg]
    wg_ref, wu_ref, wd_ref = refs[nseg:nseg + 3]
    ys = refs[nseg + 3:2 * nseg + 3]
    hs = refs[2 * nseg + 3:]
    s = pl.program_id(1)

    @pl.when(s < nf)
    def _():
        wg = wg_ref[...].astype(bf16)
        wu = wu_ref[...].astype(bf16)
        for x_ref, h_scr in zip(xs, hs):
            x = x_ref[...]
            g = jnp.dot(x, wg, preferred_element_type=f32)
            u = jnp.dot(x, wu, preferred_element_type=f32)
            h_scr[s] = ((g * jax.nn.sigmoid(g)) * u).astype(bf16)

    @pl.when(s >= nf)
    def _():
        wd = wd_ref[...].astype(bf16)
        for y_ref, h_scr in zip(ys, hs):
            h = jnp.concatenate([h_scr[f] for f in range(nf)], axis=1)
            y_ref[...] = jnp.dot(h, wd, preferred_element_type=f32)


def _expert_ffn(xs_list, w_gate, w_up, w_down, layer):
    _, ne, d, fh = w_gate.shape
    tf = 256
    tn = 512
    nf = fh // tf
    nd = d // tn
    nseg = len(xs_list)
    xspec = lambda m: pl.BlockSpec((None, m, d), lambda e, s: (e, 0, 0))
    wspec = pl.BlockSpec((None, None, d, tf), lambda e, s: (layer, e, 0, jnp.minimum(s, nf - 1)))
    dspec = pl.BlockSpec((None, None, fh, tn), lambda e, s: (layer, e, 0, jnp.maximum(s - nf, 0)))
    yspec = lambda m: pl.BlockSpec((None, m, tn), lambda e, s: (e, 0, jnp.maximum(s - nf, 0)))
    ms = [x.shape[1] for x in xs_list]
    return pl.pallas_call(
        functools.partial(_ffn_kernel, nseg=nseg, nf=nf),
        grid=(ne, nf + nd),
        in_specs=[xspec(m) for m in ms] + [wspec, wspec, dspec],
        out_specs=[yspec(m) for m in ms],
        out_shape=[jax.ShapeDtypeStruct((ne, m, d), f32) for m in ms],
        scratch_shapes=[pltpu.VMEM((nf, m, tf), bf16) for m in ms],
        compiler_params=_cparams(("parallel", "arbitrary"), VMEM_LIMIT),
        name="expert_ffn",
    )(*xs_list, w_gate, w_up, w_down)


def kernel(x, c, ctx, c_ctx, w_mod, b_mod, g_pre_mix, g_post_mix, g_pre_ffn, g_post_ffn, w_in, rpb,
           conv_w, g_mix_out, w_out, w_router, w_gate, w_up, w_down):
    depth = w_mod.shape[0]
    b, n, d = x.shape
    l_ctx = ctx.shape[1]
    ctx_row = b
    cvec = jnp.concatenate([c, c_ctx[None, :], jnp.zeros((SUBLANES - b - 1, d), f32)], axis=0)
    mod = _modulation(cvec, w_mod, b_mod)
    lat_row = lambda bi: bi
    cx_row = lambda bi: ctx_row
    tm = min(n, 512)
    for layer in range(depth):
        last = layer == depth - 1
        modl = mod[layer].reshape(SUBLANES, N_MOD, 1, d)
        w_in_b = w_in[layer].astype(bf16)
        w_out_b = w_out[layer].astype(bf16)
        w_router_t = w_router[layer].T
        q, k, _, vt, bg, z, f = _in_projection(x, modl, lat_row, g_pre_mix[layer], w_in_b, tm)
        qc, kc, vc, vct, bgc, zc, fc = _in_projection(ctx, modl, cx_row, g_pre_mix[layer], w_in_b, l_ctx)
        na = _neighbourhood_attention(q, k, vt, kc, vct, _na_bias_table(rpb[layer]))
        fo = _fourier_mix(f)
        x, h2_rows, logits = _merge(na, bg, z, fo, x, modl, lat_row, conv_w[layer], g_mix_out[layer],
                                    w_out_b, g_post_mix[layer], g_pre_ffn[layer], w_router_t, tm)
        idx, gates = _route(logits)
        xs_list = [_dispatch(h2_rows, idx, b)]
        if not last:
            na_c = _context_attention(qc, kc, vc)
            fo_c = _fourier_mix_direct(fc)
            ctx, h2c_rows, logits_c = _merge(na_c, bgc, zc, fo_c, ctx, modl, cx_row, conv_w[layer],
                                             g_mix_out[layer], w_out_b, g_post_mix[layer],
                                             g_pre_ffn[layer], w_router_t, l_ctx)
            idx_c, gates_c = _route(logits_c)
            xs_list.append(_dispatch(h2c_rows, idx_c, b))
        ys = _expert_ffn(xs_list, w_gate, w_up, w_down, layer)
        x = _combine(ys[0], idx, gates, x, modl, lat_row, g_post_ffn[layer])
        if not last:
            ctx = _combine(ys[1], idx_c, gates_c, ctx, modl, cx_row, g_post_ffn[layer])
    return x
```

```python
import functools
import math

import numpy as np
import jax
import jax.numpy as jnp
from jax import lax
from jax.experimental import pallas as pl
from jax.experimental.pallas import tpu as pltpu

f32 = jnp.float32
bf16 = jnp.bfloat16
i32 = jnp.int32

D_MODEL = 1024
GRID_W = 64
HEAD_DIM = 64
NA_WIDTH = 512
NA_HEADS = 8
NA_ROWS = 8
NA_COLS = 16
CONV_CH = 256
FOURIER_WIDTH = 256
FOURIER_GROUP_DIM = 64
N_EXPERTS = 16
CAPACITY_FACTOR = 2
N_MOD = 6
RMS_EPS = 1e-6
NEG_INF = -1e30

LANES = 128
SUBLANES = 8
ROW_TILES = D_MODEL // LANES
VMEM_LIMIT = 56 * 1024 * 1024

_NT = (((1,), (1,)), ((), ()))


def _cparams(sem, vmem=None):
    return pltpu.CompilerParams(dimension_semantics=sem, vmem_limit_bytes=vmem)


def _rms(x):
    return x * lax.rsqrt(jnp.mean(x * x, axis=-1, keepdims=True) + RMS_EPS)


def _ones_where(mask, dtype=bf16):
    return jnp.where(mask, 1.0, 0.0).astype(dtype)


def _mod_kernel(c_ref, w_ref, b_ref, o_ref):
    c = c_ref[...]
    s = c * jax.nn.sigmoid(c)
    o_ref[...] = jnp.dot(s, w_ref[...], preferred_element_type=f32,
                         precision=lax.Precision.HIGHEST) + b_ref[...]


def _modulation(cvec, w_mod, b_mod):
    depth, d, n6 = w_mod.shape
    tn = 1536
    return pl.pallas_call(
        _mod_kernel,
        grid=(depth, n6 // tn),
        in_specs=[
            pl.BlockSpec((SUBLANES, d), lambda l, j: (0, 0)),
            pl.BlockSpec((None, d, tn), lambda l, j: (l, 0, j)),
            pl.BlockSpec((None, 1, tn), lambda l, j: (l, 0, j)),
        ],
        out_specs=pl.BlockSpec((None, SUBLANES, tn), lambda l, j: (l, 0, j)),
        out_shape=jax.ShapeDtypeStruct((depth, SUBLANES, n6), f32),
        compiler_params=_cparams(("parallel", "parallel"), 40 * 1024 * 1024),
        name="modulation",
    )(cvec, w_mod, b_mod.reshape(depth, 1, n6))


def _inproj_kernel(x_ref, sh_ref, sc_ref, g_ref, w_ref, wvt_ref, q_ref, k_ref, vt_ref, bg_ref, z_ref, f_ref,
                   *maybe_v_ref):
    x = x_ref[...]
    h = (_rms(x) * g_ref[...]) * (1.0 + sc_ref[...]) + sh_ref[...]
    hb = h.astype(bf16)

    def proj(a, b):
        return jnp.dot(hb, w_ref[:, a:b], preferred_element_type=f32)

    o = 3 * NA_WIDTH
    q_ref[...] = (proj(0, NA_WIDTH) * (HEAD_DIM ** -0.5)).astype(bf16)
    k_ref[...] = proj(NA_WIDTH, 2 * NA_WIDTH).astype(bf16)
    vt_ref[...] = lax.dot_general(wvt_ref[...], hb, _NT, preferred_element_type=f32).astype(bf16)
    for v_ref in maybe_v_ref:
        v_ref[...] = proj(2 * NA_WIDTH, o).astype(bf16)
    bg_ref[...] = proj(o, o + CONV_CH)
    z_ref[...] = proj(o + CONV_CH, o + 2 * CONV_CH) * proj(o + 2 * CONV_CH, o + 3 * CONV_CH)
    f_ref[...] = proj(o + 3 * CONV_CH, o + 3 * CONV_CH + FOURIER_WIDTH).astype(bf16)


def _in_projection(x, modl, mod_row, g_pre, w_in_b, tm, with_v):
    b, n, d = x.shape
    in_w = w_in_b.shape[1]
    wvt = w_in_b[:, 2 * NA_WIDTH:3 * NA_WIDTH].T
    row = lambda c: pl.BlockSpec((None, None, 1, d), lambda bi, i: (mod_row(bi), c, 0, 0))
    tok = lambda w: pl.BlockSpec((None, tm, w), lambda bi, i: (bi, i, 0))
    shp = lambda w, dt: jax.ShapeDtypeStruct((b, n, w), dt)
    extra = [(tok(NA_WIDTH), shp(NA_WIDTH, bf16))] if with_v else []
    return pl.pallas_call(
        _inproj_kernel,
        grid=(b, n // tm),
        in_specs=[tok(d), row(0), row(1),
                  pl.BlockSpec((1, d), lambda bi, i: (0, 0)),
                  pl.BlockSpec((d, in_w), lambda bi, i: (0, 0)),
                  pl.BlockSpec((NA_WIDTH, d), lambda bi, i: (0, 0))],
        out_specs=[tok(NA_WIDTH), tok(NA_WIDTH),
                   pl.BlockSpec((None, NA_WIDTH, tm), lambda bi, i: (bi, 0, i)),
                   tok(CONV_CH), tok(CONV_CH), tok(FOURIER_WIDTH)] + [e[0] for e in extra],
        out_shape=[shp(NA_WIDTH, bf16), shp(NA_WIDTH, bf16),
                   jax.ShapeDtypeStruct((b, NA_WIDTH, n), bf16),
                   shp(CONV_CH, f32), shp(CONV_CH, f32), shp(FOURIER_WIDTH, bf16)] + [e[1] for e in extra],
        compiler_params=_cparams(("parallel", "parallel"), 40 * 1024 * 1024),
        name="in_projection",
    )(x, modl, modl, g_pre.reshape(1, d), w_in_b, wvt)


def _softmax_pv(parts):
    m = None
    for s, _ in parts:
        mi = jnp.max(s, axis=-1, keepdims=True)
        m = mi if m is None else jnp.maximum(m, mi)
    l = None
    o = None
    for s, v in parts:
        e = jnp.exp(s - m)
        li = jnp.sum(e, axis=-1, keepdims=True)
        oi = jnp.dot(e.astype(bf16), v, preferred_element_type=f32)
        l = li if l is None else l + li
        o = oi if o is None else o + oi
    return o * (1.0 / l)


def _na_kernel(q_ref, k_ref, vt_ref, kc_ref, vct_ref, bias_ref, o_ref, w_scr, *, rows, rb):
    i = pl.program_id(1)
    win = NA_ROWS * GRID_W
    npair = NA_HEADS // 2
    nwin = rows - 1
    half = NA_ROWS // 2
    unroll = 4

    @pl.when(i == 0)
    def _():
        for p in range(npair):
            vt = vt_ref[p * LANES:(p + 1) * LANES, :]
            for j in range(nwin):
                w_scr[p * nwin + j] = vt[:, j * GRID_W:j * GRID_W + LANES]

    lo = lax.broadcasted_iota(i32, (GRID_W, LANES), 1) < HEAD_DIM

    def scores(tile):
        rs, dr0, kstart, qstart, p = tile
        ls = slice(p * LANES, (p + 1) * LANES)
        qp = q_ref[pl.ds(qstart, GRID_W), ls]
        zero = jnp.zeros_like(qp)
        q2 = jnp.concatenate([jnp.where(lo, qp, zero), jnp.where(lo, zero, qp)], axis=0)
        s_w = lax.dot_general(k_ref[pl.ds(kstart, win), ls], q2, _NT, preferred_element_type=f32)
        s_c = lax.dot_general(kc_ref[:, ls], q2, _NT, preferred_element_type=f32)
        s_w = s_w + jnp.concatenate([bias_ref[p, dr0 + 2 * jj] for jj in range(half)], axis=0)
        return s_w, s_c

    def finish(tile, s_w, s_c):
        rs, dr0, kstart, qstart, p = tile
        ls = slice(p * LANES, (p + 1) * LANES)
        m = jnp.maximum(jnp.max(s_w, axis=0, keepdims=True), jnp.max(s_c, axis=0, keepdims=True))
        e_w = jnp.exp(s_w - m)
        e_c = jnp.exp(s_c - m)
        l = jnp.sum(e_w, axis=0, keepdims=True) + jnp.sum(e_c, axis=0, keepdims=True)
        vtw = jnp.concatenate([w_scr[p * nwin + rs + 2 * jj] for jj in range(half)], axis=1)
        ot = (jnp.dot(vtw, e_w.astype(bf16), preferred_element_type=f32)
              + jnp.dot(vct_ref[ls, :], e_c.astype(bf16), preferred_element_type=f32))
        of = (ot * (1.0 / l)).T
        o_ref[pl.ds(qstart, GRID_W), ls] = jnp.where(lo, of[:GRID_W], of[GRID_W:])

    def rows_body(jb, carry):
        tiles = []
        for u in range(unroll):
            j = jb * unroll + u
            r = i * rb + j
            rs = jnp.clip(r - half, 0, rows - NA_ROWS)
            dr0 = (NA_ROWS - 1) - (r - rs)
            kstart = pl.multiple_of(rs * GRID_W, GRID_W)
            qstart = pl.multiple_of(j * GRID_W, GRID_W)
            tiles += [(rs, dr0, kstart, qstart, p) for p in range(npair)]
        nxt = scores(tiles[0])
        for n, tile in enumerate(tiles):
            cur = nxt
            if n + 1 < len(tiles):
                nxt = scores(tiles[n + 1])
            finish(tile, *cur)
        return carry

    lax.fori_loop(0, rb // unroll, rows_body, 0)


def _bias_pairs_kernel(rv_ref, oh_ref, mask_ref, o_ref):
    acc = None
    for piece in _split3(rv_ref[...]):
        d = jnp.dot(piece, oh_ref[...], preferred_element_type=f32)
        acc = d if acc is None else acc + d
    o_ref[...] = acc + mask_ref[...]


def _na_bias_table(rpb_l):
    nh, ndr, ndc = rpb_l.shape
    pad = 32
    cols = np.arange(GRID_W)
    dc = np.clip(cols[None, :] - cols[:, None], -(NA_COLS - 1), NA_COLS - 1) + (NA_COLS - 1)
    col_start = np.clip(cols - NA_COLS // 2, 0, GRID_W - NA_COLS)
    col_in = (cols[None, :] >= col_start[:, None]) & (cols[None, :] < col_start[:, None] + NA_COLS)
    oh = np.zeros((2, 2, pad, 2, GRID_W, 2, GRID_W), np.float32)
    for par in range(2):
        for s in range(2):
            oh[par, s, dc, s, cols[None, :], par, cols[:, None]] = 1.0
    ncol = 2 * GRID_W * LANES
    oh = jnp.asarray(oh.reshape(4 * pad, ncol), bf16)
    mask = np.where(np.broadcast_to(col_in.T[None, :, None, :], (2, GRID_W, 2, GRID_W)), 0.0, NEG_INF)
    mask = jnp.asarray(mask.reshape(1, ncol), f32)
    rp = jnp.pad(rpb_l, ((0, 0), (0, 0), (0, pad - ndc)))
    rv = jnp.concatenate([rp[:, 0:ndr - 1], rp[:, 1:ndr]], axis=-1)
    rv = rv.reshape(nh // 2, 2, ndr - 1, 2 * pad).transpose(0, 2, 1, 3).reshape(nh // 2 * (ndr - 1), 4 * pad)
    tn = 2048
    out = pl.pallas_call(
        _bias_pairs_kernel, grid=(ncol // tn,),
        in_specs=[pl.BlockSpec(rv.shape, lambda j: (0, 0)),
                  pl.BlockSpec((4 * pad, tn), lambda j: (0, j)),
                  pl.BlockSpec((1, tn), lambda j: (0, j))],
        out_specs=pl.BlockSpec((rv.shape[0], tn), lambda j: (0, j)),
        out_shape=jax.ShapeDtypeStruct((rv.shape[0], ncol), f32),
        compiler_params=_cparams(("parallel",)), name="na_bias_table",
    )(rv, oh, mask)
    return out.reshape(nh // 2, ndr - 1, LANES, LANES)


def _neighbourhood_attention(q, k, vt, kc, vct, bias):
    b, n, w = q.shape
    l = kc.shape[1]
    rows = n // GRID_W
    rb = 8
    return pl.pallas_call(
        functools.partial(_na_kernel, rows=rows, rb=rb),
        grid=(b, rows // rb),
        in_specs=[pl.BlockSpec((None, rb * GRID_W, w), lambda bi, i: (bi, i, 0)),
                  pl.BlockSpec((None, n, w), lambda bi, i: (bi, 0, 0)),
                  pl.BlockSpec((None, w, n), lambda bi, i: (bi, 0, 0)),
                  pl.BlockSpec((None, l, w), lambda bi, i: (bi, 0, 0)),
                  pl.BlockSpec((None, w, l), lambda bi, i: (bi, 0, 0)),
                  pl.BlockSpec(bias.shape, lambda bi, i: (0, 0, 0, 0))],
        out_specs=pl.BlockSpec((None, rb * GRID_W, w), lambda bi, i: (bi, i, 0)),
        out_shape=jax.ShapeDtypeStruct((b, n, w), f32),
        scratch_shapes=[pltpu.VMEM((NA_HEADS // 2 * (rows - 1), LANES, LANES), bf16)],
        compiler_params=_cparams(("parallel", "arbitrary"), 48 * 1024 * 1024),
        name="neighbourhood_attention",
    )(q, k, vt, kc, vct, bias)


def _ctx_attn_kernel(q_ref, k_ref, v_ref, o_ref):
    m = q_ref.shape[0]
    lo = lax.broadcasted_iota(i32, (m, LANES), 1) < HEAD_DIM
    for p in range(NA_HEADS // 2):
        ls = slice(p * LANES, (p + 1) * LANES)
        qp, kp, vp = q_ref[:, ls], k_ref[:, ls], v_ref[:, ls]
        outs = []
        for par in range(2):
            qm = jnp.where(lo if par == 0 else jnp.logical_not(lo), qp, jnp.zeros_like(qp))
            s = lax.dot_general(qm, kp, _NT, preferred_element_type=f32)
            outs.append(_softmax_pv([(s, vp)]))
        o_ref[:, ls] = jnp.where(lo, outs[0], outs[1])


def _context_attention(q, k, v):
    b, l, w = q.shape
    spec = pl.BlockSpec((None, l, w), lambda bi: (bi, 0, 0))
    return pl.pallas_call(
        _ctx_attn_kernel, grid=(b,), in_specs=[spec, spec, spec], out_specs=spec,
        out_shape=jax.ShapeDtypeStruct((b, l, w), f32),
        compiler_params=_cparams(("parallel",)), name="context_attention",
    )(q, k, v)


def _dft_tables_channel():
    c = np.arange(FOURIER_GROUP_DIM)
    ang = 2.0 * np.pi * np.outer(c, c) / FOURIER_GROUP_DIM
    g = FOURIER_WIDTH // FOURIER_GROUP_DIM
    cb = np.kron(np.eye(g), np.cos(ang))
    sb = np.kron(np.eye(g), np.sin(ang))
    return np.concatenate([cb, -sb], axis=1)


def _fourier_kernel(f_ref, cs_ref, d1_ref, d3_ref, tc_ref, ts_ref, o_ref, g_scr, br_scr, bi_scr, o_scr, *, r):
    nt = FOURIER_WIDTH // LANES
    g = jnp.dot(f_ref[...], cs_ref[...].astype(bf16), preferred_element_type=f32)
    for c in range(2 * nt):
        g_scr[c] = g[:, c * LANES:(c + 1) * LANES]
    d1 = d1_ref[...].astype(bf16)

    def strided(scr, tiles, start, size, stride):
        return jnp.concatenate([scr[c, pl.ds(start, size, stride=stride), :] for c in tiles], axis=1)

    for n2 in range(GRID_W):
        gr = strided(g_scr, range(nt), n2, r, GRID_W)
        gi = strided(g_scr, range(nt, 2 * nt), n2, r, GRID_W)
        rhs = jnp.concatenate([gr, gi], axis=0).astype(bf16)
        a = jnp.dot(d1, rhs, preferred_element_type=f32)
        ar, ai = a[:r], a[r:]
        tc = tc_ref[n2 * r:(n2 + 1) * r, :]
        ts = ts_ref[n2 * r:(n2 + 1) * r, :]
        br = ar * tc + ai * ts
        bi = ai * tc - ar * ts
        for c in range(nt):
            br_scr[c, n2 * r:(n2 + 1) * r, :] = br[:, c * LANES:(c + 1) * LANES]
            bi_scr[c, n2 * r:(n2 + 1) * r, :] = bi[:, c * LANES:(c + 1) * LANES]
    d3 = d3_ref[...].astype(bf16)
    for k1 in range(r):
        br = strided(br_scr, range(nt), k1, GRID_W, r)
        bi = strided(bi_scr, range(nt), k1, GRID_W, r)
        rhs = jnp.concatenate([br, bi], axis=0).astype(bf16)
        y = jnp.dot(d3, rhs, preferred_element_type=f32)
        for c in range(nt):
            o_scr[c, pl.ds(k1, GRID_W, stride=r), :] = y[:, c * LANES:(c + 1) * LANES]
    for c in range(nt):
        o_ref[:, c * LANES:(c + 1) * LANES] = o_scr[c]


def _fourier_mix(f):
    b, n, w = f.shape
    r = n // GRID_W
    cs = jnp.asarray(_dft_tables_channel(), f32)
    a1 = 2.0 * np.pi * np.outer(np.arange(r), np.arange(r)) / r
    c1, s1 = np.cos(a1), np.sin(a1)
    d1 = jnp.asarray(np.block([[c1, s1], [-s1, c1]]), f32)
    a3 = 2.0 * np.pi * np.outer(np.arange(GRID_W), np.arange(GRID_W)) / GRID_W
    d3 = jnp.asarray(np.concatenate([np.cos(a3), np.sin(a3)], axis=1), f32)
    n2 = np.arange(GRID_W)[:, None]
    k1 = np.arange(r)[None, :]
    at = 2.0 * np.pi * ((n2 * k1) % n) / n
    scale = 1.0 / math.sqrt(n * FOURIER_GROUP_DIM)
    tc = jnp.asarray(np.broadcast_to((np.cos(at) * scale).reshape(n, 1), (n, w)), f32)
    ts = jnp.asarray(np.broadcast_to((np.sin(at) * scale).reshape(n, 1), (n, w)), f32)
    const = lambda a: pl.BlockSpec(a.shape, lambda bi: (0,) * a.ndim)
    return pl.pallas_call(
        functools.partial(_fourier_kernel, r=r),
        grid=(b,),
        in_specs=[pl.BlockSpec((None, n, w), lambda bi: (bi, 0, 0)),
                  const(cs), const(d1), const(d3), const(tc), const(ts)],
        out_specs=pl.BlockSpec((None, n, w), lambda bi: (bi, 0, 0)),
        out_shape=jax.ShapeDtypeStruct((b, n, w), f32),
        scratch_shapes=[pltpu.VMEM((2 * w // LANES, n, LANES), f32)] + [pltpu.VMEM((w // LANES, n, LANES), f32)] * 3,
        compiler_params=_cparams(("parallel",), 48 * 1024 * 1024),
        name="fourier_mix",
    )(f, cs, d1, d3, tc, ts)


def _fourier_direct_kernel(f_ref, cs_ref, dn_ref, o_ref):
    w = FOURIER_WIDTH
    g = jnp.dot(f_ref[...], cs_ref[...].astype(bf16), preferred_element_type=f32)
    rhs = jnp.concatenate([g[:, :w], g[:, w:]], axis=0).astype(bf16)
    o_ref[...] = jnp.dot(dn_ref[...].astype(bf16), rhs, preferred_element_type=f32)


def _fourier_mix_direct(f):
    b, n, w = f.shape
    cs = jnp.asarray(_dft_tables_channel(), f32)
    a = 2.0 * np.pi * (np.outer(np.arange(n), np.arange(n)) % n) / n
    scale = 1.0 / math.sqrt(n * FOURIER_GROUP_DIM)
    dn = jnp.asarray(np.concatenate([np.cos(a), np.sin(a)], axis=1) * scale, f32)
    const = lambda t: pl.BlockSpec(t.shape, lambda bi: (0,) * t.ndim)
    return pl.pallas_call(
        _fourier_direct_kernel, grid=(b,),
        in_specs=[pl.BlockSpec((None, n, w), lambda bi: (bi, 0, 0)), const(cs), const(dn)],
        out_specs=pl.BlockSpec((None, n, w), lambda bi: (bi, 0, 0)),
        out_shape=jax.ShapeDtypeStruct((b, n, w), f32),
        compiler_params=_cparams(("parallel",)), name="fourier_mix_direct",
    )(f, cs, dn)


def _to_rows(t):
    st = jnp.stack([t[:, c * LANES:(c + 1) * LANES] for c in range(ROW_TILES)], axis=0)
    return jnp.swapaxes(st, 0, 1)


def _from_rows(rows):
    t = jnp.swapaxes(rows, 0, 1)
    return [t[c] for c in range(ROW_TILES)]


def _merge_kernel(na_ref, bg_ref, z_ref, zp_ref, zn_ref, fo_ref, x_ref, cw_ref, gmix_ref, wout_ref,
                  gpost_ref, g1_ref, sh2_ref, sc2_ref, gpre_ref, wr_ref,
                  xo_ref, h2_ref, lg_ref):
    i = pl.program_id(1)
    nt = pl.num_programs(1)
    tm = z_ref.shape[0]
    z = z_ref[...]
    zp = jnp.where(i > 0, zp_ref[SUBLANES - 1:SUBLANES, :], 0.0)
    zn = jnp.where(i < nt - 1, zn_ref[0:1, :], 0.0)
    rid = lax.broadcasted_iota(i32, z.shape, 0)
    z_dn = jnp.where(rid == 0, zp, pltpu.roll(z, 1, axis=0))
    z_up = jnp.where(rid == tm - 1, zn, pltpu.roll(z, tm - 1, axis=0))
    cv = bg_ref[...] * (cw_ref[0:1, :] * z_dn + cw_ref[1:2, :] * z + cw_ref[2:3, :] * z_up)

    a, b_ = NA_WIDTH, NA_WIDTH + CONV_CH
    parts = ((na_ref[...], 0, a), (cv, a, b_), (fo_ref[...], b_, D_MODEL))
    m = None
    for t, lo, hi in parts:
        y = (_rms(t) * gmix_ref[:, lo:hi]).astype(bf16)
        d = jnp.dot(y, wout_ref[lo:hi, :], preferred_element_type=f32)
        m = d if m is None else m + d
    xn = x_ref[...] + g1_ref[...] * (_rms(m) * gpost_ref[...])
    xo_ref[...] = xn
    h2 = (_rms(xn) * gpre_ref[...]) * (1.0 + sc2_ref[...]) + sh2_ref[...]
    h2_ref[...] = _to_rows(h2)
    hh = h2.astype(bf16)
    hl = (h2 - hh.astype(f32)).astype(bf16)
    wr = wr_ref[...]
    wh = wr.astype(bf16)
    wl = (wr - wh.astype(f32)).astype(bf16)
    nt_dot = lambda u, v: lax.dot_general(u, v, _NT, preferred_element_type=f32)
    lg_ref[...] = nt_dot(wh, hh) + (nt_dot(wh, hl) + nt_dot(wl, hh))


def _merge(na, bg, z, fo, x, modl, mod_row, conv_w, g_mix, w_out_b, g_post, g_pre_ffn, w_router_t, tm):
    b, n, d = x.shape
    hb = tm // SUBLANES
    nhb = n // SUBLANES
    row = lambda c: pl.BlockSpec((None, None, 1, d), lambda bi, i: (mod_row(bi), c, 0, 0))
    tok = lambda w: pl.BlockSpec((None, tm, w), lambda bi, i: (bi, i, 0))
    const = lambda a: pl.BlockSpec(a.shape, lambda bi, i: (0,) * a.ndim)
    vec = lambda a: a.reshape(1, -1)
    nt = n // tm
    args = (na, bg, z, z, z, fo, x, conv_w, vec(g_mix), w_out_b, vec(g_post), modl, modl, modl,
            vec(g_pre_ffn), w_router_t)
    return pl.pallas_call(
        _merge_kernel,
        grid=(b, nt),
        in_specs=[tok(NA_WIDTH), tok(CONV_CH), tok(CONV_CH),
                  pl.BlockSpec((None, SUBLANES, CONV_CH), lambda bi, i: (bi, jnp.maximum(i * hb - 1, 0), 0)),
                  pl.BlockSpec((None, SUBLANES, CONV_CH), lambda bi, i: (bi, jnp.minimum((i + 1) * hb, nhb - 1), 0)),
                  tok(FOURIER_WIDTH), tok(d), const(conv_w), const(vec(g_mix)), const(w_out_b),
                  const(vec(g_post)), row(2), row(3), row(4), const(vec(g_pre_ffn)), const(w_router_t)],
        out_specs=[tok(d),
                   pl.BlockSpec((tm, ROW_TILES, LANES), lambda bi, i: (bi * nt + i, 0, 0)),
                   pl.BlockSpec((None, N_EXPERTS, tm), lambda bi, i: (bi, 0, i))],
        out_shape=[jax.ShapeDtypeStruct((b, n, d), f32),
                   jax.ShapeDtypeStruct((b * n, ROW_TILES, LANES), f32),
                   jax.ShapeDtypeStruct((b, N_EXPERTS, n), f32)],
        compiler_params=_cparams(("parallel", "parallel"), 40 * 1024 * 1024),
        name="merge",
    )(*args)


def _split3(a):
    h = a.astype(bf16)
    r1 = a - h.astype(f32)
    m = r1.astype(bf16)
    l = (r1 - m.astype(f32)).astype(bf16)
    return h, m, l


def _threshold_search(aff, count, cap):
    t = jnp.zeros(aff.shape, i32)
    for lo_bit in range(27, -1, -3):
        digit = jnp.zeros(aff.shape, i32)
        for j in range(1, 8):
            cand = lax.bitcast_convert_type(t | (j << lo_bit), f32)
            digit = digit + jnp.where(count(aff >= cand) >= cap, 1, 0)
        t = t | (digit << lo_bit)
    return lax.bitcast_convert_type(t, f32)


def _route_kernel(lg_ref, o_ref, *, nch, cap):
    ne = N_EXPERTS
    r = ne * nch
    sh = nch.bit_length() - 1
    lg = lg_ref[...]
    ex = jnp.exp(lg - jnp.max(lg, axis=0, keepdims=True))
    aff = (ex / jnp.sum(ex, axis=0, keepdims=True)).reshape(r, LANES)

    li = lax.broadcasted_iota(i32, (LANES, LANES), 0)
    lj = lax.broadcasted_iota(i32, (LANES, LANES), 1)
    triu = _ones_where(li <= lj)
    tril = _ones_where(lj <= li)
    eye = _ones_where(li == lj)
    ones = jnp.ones((LANES, LANES), bf16)
    ri = lax.broadcasted_iota(i32, (r, r), 0)
    rj = lax.broadcasted_iota(i32, (r, r), 1)
    same = (ri >> sh) == (rj >> sh)
    sl =_ones_where(same & (rj < ri))
    mm = lambda u, v: jnp.dot(u, v, preferred_element_type=f32)

    vpe = nch // SUBLANES
    e8i = lax.broadcasted_iota(i32, (ne * SUBLANES, ne * SUBLANES), 0) >> 3
    e8j = lax.broadcasted_iota(i32, (ne * SUBLANES, ne * SUBLANES), 1) >> 3
    bd8 = _ones_where(e8i == e8j)

    def count(mask):
        m = jnp.where(mask, 1.0, 0.0).reshape(ne, vpe, SUBLANES, LANES)
        m = jnp.sum(m, axis=1).reshape(ne * SUBLANES, LANES).astype(bf16)
        tot = mm(mm(bd8, m).astype(bf16), ones)
        tot = jnp.broadcast_to(tot.reshape(ne, 1, SUBLANES, LANES), (ne, vpe, SUBLANES, LANES))
        return tot.reshape(r, LANES)

    thr = _threshold_search(aff, count, cap)
    gt = aff > thr
    eq = aff == thr
    need = cap - count(gt)
    eqb = _ones_where(eq)
    pe = mm(eqb, triu) + mm(sl, mm(eqb, ones).astype(bf16))
    sel = gt | (eq & (pe <= need))
    selb = _ones_where(sel)
    rt = mm(selb, ones)
    oin = mm(sl, rt.astype(bf16)) + rt

    reps = cap // LANES
    tile = lambda a: jnp.concatenate([a] * reps, axis=1) if reps > 1 else a
    s_f = lax.broadcasted_iota(i32, (nch, cap), 1).astype(f32)
    s8 = lax.broadcasted_iota(i32, (SUBLANES, cap), 1).astype(f32)
    ji = lax.broadcasted_iota(i32, (nch, cap), 0).astype(f32)
    lis = lax.broadcasted_iota(i32, (LANES, cap), 0).astype(f32)
    ri8 = lax.broadcasted_iota(i32, (SUBLANES, cap), 0)
    ones8c = jnp.ones((SUBLANES, nch), bf16)
    ones8l = jnp.ones((SUBLANES, LANES), bf16)
    for e in range(ne):
        rows = slice(e * nch, (e + 1) * nch)
        zmask = tile(oin[rows]) <= s_f
        js = mm(ones8c, _ones_where(zmask))
        oex = mm(ones8c, jnp.where(zmask, tile(rt[rows]), 0.0).astype(bf16))
        s_loc = s8 - oex
        u = _ones_where(ji == jnp.broadcast_to(js[0:1], (nch, cap)))
        plt = lax.dot_general(tril, selb[rows], _NT, preferred_element_type=f32)
        ptg = mm(plt.astype(bf16), u)
        c = _ones_where(ptg <= jnp.broadcast_to(s_loc[0:1], (LANES, cap)))
        tl = mm(ones8l, c)
        idx = js * float(LANES) + tl
        affg = None
        for piece in _split3(aff[rows]):
            pt = lax.dot_general(eye, piece, _NT, preferred_element_type=f32).astype(bf16)
            pg = mm(pt, u)
            affg = pg if affg is None else affg + pg
        hit = lis == jnp.broadcast_to(tl[0:1], (LANES, cap))
        gate = jnp.sum(jnp.where(hit, affg, 0.0), axis=0, keepdims=True)
        o_ref[e] = jnp.where(ri8 == 0, idx, jnp.where(ri8 == 1, jnp.broadcast_to(gate, (SUBLANES, cap)), 0.0))


def _route_small_kernel(lg_ref, o_ref, *, cap):
    ne, n = lg_ref.shape
    lg = lg_ref[...]
    ex = jnp.exp(lg - jnp.max(lg, axis=0, keepdims=True))
    aff = ex / jnp.sum(ex, axis=0, keepdims=True)
    li = lax.broadcasted_iota(i32, (n, n), 0)
    lj = lax.broadcasted_iota(i32, (n, n), 1)
    triu = _ones_where(li <= lj)
    ones = jnp.ones((n, n), bf16)
    mm = lambda u, v: jnp.dot(u, v, preferred_element_type=f32)
    count = lambda mask: mm(_ones_where(mask), ones)
    thr = _threshold_search(aff, count, cap)
    gt = aff > thr
    eq = aff == thr
    need = cap - count(gt)
    sel = gt | (eq & (mm(_ones_where(eq), triu) <= need))
    slot = jnp.where(sel, mm(_ones_where(sel), triu) - 1.0, -1.0)
    tid = lax.broadcasted_iota(i32, (SUBLANES, n), 1).astype(f32)
    ri = lax.broadcasted_iota(i32, (SUBLANES, n), 0)
    s_i = lax.broadcasted_iota(i32, (cap, n), 0).astype(f32)
    ro = lax.broadcasted_iota(i32, (SUBLANES, cap), 0)
    bc = lambda a: jnp.broadcast_to(a, (SUBLANES, n))
    for e in range(ne):
        et = _ones_where(s_i == jnp.broadcast_to(slot[e:e + 1], (cap, n)))
        h, m, l = _split3(aff[e:e + 1])
        rm = jnp.where(ri == 0, tid, jnp.where(ri == 1, bc(h.astype(f32)), jnp.where(
            ri == 2, bc(m.astype(f32)), jnp.where(ri == 3, bc(l.astype(f32)), 0.0)))).astype(bf16)
        res = lax.dot_general(rm, et, _NT, preferred_element_type=f32)
        gate = res[1:2] + res[2:3] + res[3:4]
        o_ref[e] = jnp.where(ro == 0, jnp.broadcast_to(res[0:1], (SUBLANES, cap)),
                             jnp.where(ro == 1, jnp.broadcast_to(gate, (SUBLANES, cap)), 0.0))


def _route(logits_t):
    b, ne, n = logits_t.shape
    cap = CAPACITY_FACTOR * n // N_EXPERTS
    out_shape = jax.ShapeDtypeStruct((b, ne, SUBLANES, cap), f32)
    out_spec = pl.BlockSpec((None, ne, SUBLANES, cap), lambda bi: (bi, 0, 0, 0))
    nch = n // LANES
    if nch % SUBLANES == 0 and cap % LANES == 0:
        res = pl.pallas_call(
            functools.partial(_route_kernel, nch=nch, cap=cap), grid=(b,),
            in_specs=[pl.BlockSpec((None, ne, nch, LANES), lambda bi: (bi, 0, 0, 0))],
            out_specs=out_spec, out_shape=out_shape,
            compiler_params=_cparams(("parallel",), 40 * 1024 * 1024), name="route",
        )(logits_t.reshape(b, ne, nch, LANES))
    else:
        res = pl.pallas_call(
            functools.partial(_route_small_kernel, cap=cap), grid=(b,),
            in_specs=[pl.BlockSpec((None, ne, n), lambda bi: (bi, 0, 0))],
            out_specs=out_spec, out_shape=out_shape,
            compiler_params=_cparams(("parallel",)), name="route_small",
        )(logits_t)
    idx = res[:, :, 0, :].astype(i32).reshape(b * ne, 1, cap)
    gates = res[:, :, 1, :].reshape(b * ne, 1, cap)
    return idx, gates


def _dispatch_kernel(idx_ref, h_ref, o_ref, scr):
    cap = o_ref.shape[0]

    def body(i, c):
        scr[i] = h_ref[idx_ref[0, 0, i]]
        return c

    lax.fori_loop(0, cap, body, 0, unroll=8)
    tiles = _from_rows(scr[...])
    for c in range(ROW_TILES):
        o_ref[:, c * LANES:(c + 1) * LANES] = tiles[c].astype(bf16)


def _dispatch(h2_rows, idx, b):
    n = h2_rows.shape[0] // b
    cap = idx.shape[2]
    ne = N_EXPERTS
    return pl.pallas_call(
        _dispatch_kernel,
        grid=(b, ne),
        in_specs=[pl.BlockSpec((1, 1, cap), lambda bi, e: (bi * ne + e, 0, 0), memory_space=pltpu.SMEM),
                  pl.BlockSpec((n, ROW_TILES, LANES), lambda bi, e: (bi, 0, 0))],
        out_specs=pl.BlockSpec((None, cap, D_MODEL), lambda bi, e: (e, bi, 0)),
        out_shape=jax.ShapeDtypeStruct((ne, b * cap, D_MODEL), bf16),
        scratch_shapes=[pltpu.VMEM((cap, ROW_TILES, LANES), f32)],
        compiler_params=_cparams(("parallel", "arbitrary"), 48 * 1024 * 1024),
        name="dispatch",
    )(idx, h2_rows)


def _combine_kernel(idx_ref, g_ref, y_ref, x_ref, g2_ref, gp_ref, o_ref, acc, ys, *, chunk):
    s = pl.program_id(1)
    ne = N_EXPERTS
    cap = y_ref.shape[0]

    @pl.when(s == 0)
    def _():
        acc[...] = jnp.zeros_like(acc)

    @pl.when(s < ne)
    def _():
        ys[...] = _to_rows(y_ref[...].astype(f32))

        group = 8

        def body(gi, c):
            base = gi * group
            ts = [idx_ref[0, 0, base + k] for k in range(group)]
            vals = [acc[ts[k]] + g_ref[0, 0, base + k] * ys[base + k] for k in range(group)]
            for k in range(group):
                acc[ts[k]] = vals[k]
            return c

        lax.fori_loop(0, cap // group, body, 0, unroll=2)

    @pl.when(s >= ne)
    def _():
        start = pl.multiple_of((s - ne) * chunk, chunk)
        tiles = _from_rows(acc[pl.ds(start, chunk)])
        ss = None
        for t in tiles:
            q = jnp.sum(t * t, axis=-1, keepdims=True)
            ss = q if ss is None else ss + q
        inv = lax.rsqrt(ss * (1.0 / D_MODEL) + RMS_EPS)
        for c in range(ROW_TILES):
            cs = slice(c * LANES, (c + 1) * LANES)
            o_ref[:, cs] = x_ref[:, cs] + g2_ref[:, cs] * ((tiles[c] * inv) * gp_ref[:, cs])


def _combine(y, idx, gates, x, modl, mod_row, g_post):
    b, n, d = x.shape
    ne = N_EXPERTS
    cap = idx.shape[2]
    chunk = min(n, 512)
    nchunk = n // chunk
    last = ne - 1
    sm = lambda: pl.BlockSpec((1, 1, cap), lambda bi, s: (bi * ne + jnp.minimum(s, last), 0, 0),
                              memory_space=pltpu.SMEM)
    return pl.pallas_call(
        functools.partial(_combine_kernel, chunk=chunk),
        grid=(b, ne + nchunk),
        in_specs=[sm(), sm(),
                  pl.BlockSpec((None, cap, D_MODEL), lambda bi, s: (jnp.minimum(s, last), bi, 0)),
                  pl.BlockSpec((None, chunk, d), lambda bi, s: (bi, jnp.maximum(s - ne, 0), 0)),
                  pl.BlockSpec((None, None, 1, d), lambda bi, s: (mod_row(bi), N_MOD - 1, 0, 0)),
                  pl.BlockSpec((1, d), lambda bi, s: (0, 0))],
        out_specs=pl.BlockSpec((None, chunk, D_MODEL), lambda bi, s: (bi, jnp.maximum(s - ne, 0), 0)),
        out_shape=jax.ShapeDtypeStruct((b, n, D_MODEL), f32),
        scratch_shapes=[pltpu.VMEM((n, ROW_TILES, LANES), f32), pltpu.VMEM((cap, ROW_TILES, LANES), f32)],
        compiler_params=_cparams(("parallel", "arbitrary"), 48 * 1024 * 1024),
        name="combine",
    )(idx, gates, y, x, modl, g_post.reshape(1, d))


def _ffn_kernel(*refs, nseg, nf):
    xs = refs[:nseg]
    wg_ref, wu_ref, wd_ref = refs[nseg:nseg + 3]
    ys = refs[nseg + 3:2 * nseg + 3]
    hs = refs[2 * nseg + 3:]
    s = pl.program_id(1)

    @pl.when(s < nf)
    def _():
        wg = wg_ref[...].astype(bf16)
        wu = wu_ref[...].astype(bf16)
        for x_ref, h_scr in zip(xs, hs):
            x = x_ref[...]
            g = jnp.dot(x, wg, preferred_element_type=f32)
            u = jnp.dot(x, wu, preferred_element_type=f32)
            h_scr[s] = ((g * jax.nn.sigmoid(g)) * u).astype(bf16)

    @pl.when(s >= nf)
    def _():
        wd = wd_ref[...].astype(bf16)
        for y_ref, h_scr in zip(ys, hs):
            h = jnp.concatenate([h_scr[f] for f in range(nf)], axis=1)
            y_ref[...] = jnp.dot(h, wd, preferred_element_type=f32)


def _expert_ffn(xs_list, w_gate, w_up, w_down, layer):
    _, ne, d, fh = w_gate.shape
    tf = 256
    tn = 512
    nf = fh // tf
    nd = d // tn
    nseg = len(xs_list)
    xspec = lambda m: pl.BlockSpec((None, m, d), lambda e, s: (e, 0, 0))
    wspec = pl.BlockSpec((None, None, d, tf), lambda e, s: (layer, e, 0, jnp.minimum(s, nf - 1)))
    dspec = pl.BlockSpec((None, None, fh, tn), lambda e, s: (layer, e, 0, jnp.maximum(s - nf, 0)))
    yspec = lambda m: pl.BlockSpec((None, m, tn), lambda e, s: (e, 0, jnp.maximum(s - nf, 0)))
    ms = [x.shape[1] for x in xs_list]
    return pl.pallas_call(
        functools.partial(_ffn_kernel, nseg=nseg, nf=nf),
        grid=(ne, nf + nd),
        in_specs=[xspec(m) for m in ms] + [wspec, wspec, dspec],
        out_specs=[yspec(m) for m in ms],
        out_shape=[jax.ShapeDtypeStruct((ne, m, d), f32) for m in ms],
        scratch_shapes=[pltpu.VMEM((nf, m, tf), bf16) for m in ms],
        compiler_params=_cparams(("parallel", "arbitrary"), VMEM_LIMIT),
        name="expert_ffn",
    )(*xs_list, w_gate, w_up, w_down)


def kernel(x, c, ctx, c_ctx, w_mod, b_mod, g_pre_mix, g_post_mix, g_pre_ffn, g_post_ffn, w_in, rpb,
           conv_w, g_mix_out, w_out, w_router, w_gate, w_up, w_down):
    depth = w_mod.shape[0]
    b, n, d = x.shape
    l_ctx = ctx.shape[1]
    ctx_row = b
    cvec = jnp.concatenate([c, c_ctx[None, :], jnp.zeros((SUBLANES - b - 1, d), f32)], axis=0)
    mod = _modulation(cvec, w_mod, b_mod)
    lat_row = lambda bi: bi
    cx_row = lambda bi: ctx_row
    tm = min(n, 512)
    for layer in range(depth):
        last = layer == depth - 1
        modl = mod[layer].reshape(SUBLANES, N_MOD, 1, d)
        w_in_b = w_in[layer].astype(bf16)
        w_out_b = w_out[layer].astype(bf16)
        w_router_t = w_router[layer].T
        q, k, vt, bg, z, f = _in_projection(x, modl, lat_row, g_pre_mix[layer], w_in_b, tm, False)
        qc, kc, vct, bgc, zc, fc, vc = _in_projection(ctx, modl, cx_row, g_pre_mix[layer], w_in_b, l_ctx, True)
        na = _neighbourhood_attention(q, k, vt, kc, vct, _na_bias_table(rpb[layer]))
        fo = _fourier_mix(f)
        x, h2_rows, logits = _merge(na, bg, z, fo, x, modl, lat_row, conv_w[layer], g_mix_out[layer],
                                    w_out_b, g_post_mix[layer], g_pre_ffn[layer], w_router_t, tm)
        idx, gates = _route(logits)
        xs_list = [_dispatch(h2_rows, idx, b)]
        if not last:
            na_c = _context_attention(qc, kc, vc)
            fo_c = _fourier_mix_direct(fc)
            ctx, h2c_rows, logits_c = _merge(na_c, bgc, zc, fo_c, ctx, modl, cx_row, conv_w[layer],
                                             g_mix_out[layer], w_out_b, g_post_mix[layer],
                                             g_pre_ffn[layer], w_router_t, l_ctx)
            idx_c, gates_c = _route(logits_c)
            xs_list.append(_dispatch(h2c_rows, idx_c, b))
        ys = _expert_ffn(xs_list, w_gate, w_up, w_down, layer)
        x = _combine(ys[0], idx, gates, x, modl, lat_row, g_post_ffn[layer])
        if not last:
            ctx = _combine(ys[1], idx_c, gates_c, ctx, modl, cx_row, g_post_ffn[layer])
    return x
```

```python
import functools
import math

import numpy as np
import jax
import jax.numpy as jnp
from jax import lax
from jax.experimental import pallas as pl
from jax.experimental.pallas import tpu as pltpu

f32 = jnp.float32
bf16 = jnp.bfloat16
i32 = jnp.int32

D_MODEL = 1024
GRID_W = 64
HEAD_DIM = 64
NA_WIDTH = 512
NA_HEADS = 8
NA_ROWS = 8
NA_COLS = 16
CONV_CH = 256
FOURIER_WIDTH = 256
FOURIER_GROUP_DIM = 64
N_EXPERTS = 16
CAPACITY_FACTOR = 2
N_MOD = 6
RMS_EPS = 1e-6
NEG_INF = -1e30

LANES = 128
SUBLANES = 8
ROW_TILES = D_MODEL // LANES
VMEM_LIMIT = 56 * 1024 * 1024

_NT = (((1,), (1,)), ((), ()))


def _cparams(sem, vmem=None):
    return pltpu.CompilerParams(dimension_semantics=sem, vmem_limit_bytes=vmem)


def _rms(x):
    return x * lax.rsqrt(jnp.mean(x * x, axis=-1, keepdims=True) + RMS_EPS)


def _ones_where(mask, dtype=bf16):
    return jnp.where(mask, 1.0, 0.0).astype(dtype)


def _mod_kernel(c_ref, w_ref, b_ref, o_ref):
    c = c_ref[...]
    s = c * jax.nn.sigmoid(c)
    o_ref[...] = jnp.dot(s, w_ref[...], preferred_element_type=f32,
                         precision=lax.Precision.HIGHEST) + b_ref[...]


def _modulation(cvec, w_mod, b_mod):
    depth, d, n6 = w_mod.shape
    tn = 1536
    return pl.pallas_call(
        _mod_kernel,
        grid=(depth, n6 // tn),
        in_specs=[
            pl.BlockSpec((SUBLANES, d), lambda l, j: (0, 0)),
            pl.BlockSpec((None, d, tn), lambda l, j: (l, 0, j)),
            pl.BlockSpec((None, 1, tn), lambda l, j: (l, 0, j)),
        ],
        out_specs=pl.BlockSpec((None, SUBLANES, tn), lambda l, j: (l, 0, j)),
        out_shape=jax.ShapeDtypeStruct((depth, SUBLANES, n6), f32),
        compiler_params=_cparams(("parallel", "parallel"), 40 * 1024 * 1024),
        name="modulation",
    )(cvec, w_mod, b_mod.reshape(depth, 1, n6))


def _inproj_kernel(x_ref, sh_ref, sc_ref, g_ref, w_ref, wvt_ref, q_ref, k_ref, vt_ref, bg_ref, z_ref, f_ref,
                   *maybe_v_ref):
    x = x_ref[...]
    h = (_rms(x) * g_ref[...]) * (1.0 + sc_ref[...]) + sh_ref[...]
    hb = h.astype(bf16)

    def proj(a, b):
        return jnp.dot(hb, w_ref[:, a:b], preferred_element_type=f32)

    o = 3 * NA_WIDTH
    q_ref[...] = (proj(0, NA_WIDTH) * (HEAD_DIM ** -0.5)).astype(bf16)
    k_ref[...] = proj(NA_WIDTH, 2 * NA_WIDTH).astype(bf16)
    vt_ref[...] = lax.dot_general(wvt_ref[...], hb, _NT, preferred_element_type=f32).astype(bf16)
    for v_ref in maybe_v_ref:
        v_ref[...] = proj(2 * NA_WIDTH, o).astype(bf16)
    bg_ref[...] = proj(o, o + CONV_CH)
    z_ref[...] = proj(o + CONV_CH, o + 2 * CONV_CH) * proj(o + 2 * CONV_CH, o + 3 * CONV_CH)
    f_ref[...] = proj(o + 3 * CONV_CH, o + 3 * CONV_CH + FOURIER_WIDTH).astype(bf16)


def _in_projection(x, modl, mod_row, g_pre, w_in_b, tm, with_v):
    b, n, d = x.shape
    in_w = w_in_b.shape[1]
    wvt = w_in_b[:, 2 * NA_WIDTH:3 * NA_WIDTH].T
    row = lambda c: pl.BlockSpec((None, None, 1, d), lambda bi, i: (mod_row(bi), c, 0, 0))
    tok = lambda w: pl.BlockSpec((None, tm, w), lambda bi, i: (bi, i, 0))
    shp = lambda w, dt: jax.ShapeDtypeStruct((b, n, w), dt)
    extra = [(tok(NA_WIDTH), shp(NA_WIDTH, bf16))] if with_v else []
    return pl.pallas_call(
        _inproj_kernel,
        grid=(b, n // tm),
        in_specs=[tok(d), row(0), row(1),
                  pl.BlockSpec((1, d), lambda bi, i: (0, 0)),
                  pl.BlockSpec((d, in_w), lambda bi, i: (0, 0)),
                  pl.BlockSpec((NA_WIDTH, d), lambda bi, i: (0, 0))],
        out_specs=[tok(NA_WIDTH), tok(NA_WIDTH),
                   pl.BlockSpec((None, NA_WIDTH, tm), lambda bi, i: (bi, 0, i)),
                   tok(CONV_CH), tok(CONV_CH), tok(FOURIER_WIDTH)] + [e[0] for e in extra],
        out_shape=[shp(NA_WIDTH, bf16), shp(NA_WIDTH, bf16),
                   jax.ShapeDtypeStruct((b, NA_WIDTH, n), bf16),
                   shp(CONV_CH, f32), shp(CONV_CH, f32), shp(FOURIER_WIDTH, bf16)] + [e[1] for e in extra],
        compiler_params=_cparams(("parallel", "parallel"), 40 * 1024 * 1024),
        name="in_projection",
    )(x, modl, modl, g_pre.reshape(1, d), w_in_b, wvt)


def _softmax_pv(parts):
    m = None
    for s, _ in parts:
        mi = jnp.max(s, axis=-1, keepdims=True)
        m = mi if m is None else jnp.maximum(m, mi)
    l = None
    o = None
    for s, v in parts:
        e = jnp.exp(s - m)
        li = jnp.sum(e, axis=-1, keepdims=True)
        oi = jnp.dot(e.astype(bf16), v, preferred_element_type=f32)
        l = li if l is None else l + li
        o = oi if o is None else o + oi
    return o * (1.0 / l)


def _na_kernel(q_ref, k_ref, vt_ref, kc_ref, vct_ref, bias_ref, o_ref, w_scr, *, rows, rb):
    i = pl.program_id(1)
    win = NA_ROWS * GRID_W
    npair = NA_HEADS // 2
    nwin = rows - 1
    half = NA_ROWS // 2
    unroll = 4

    @pl.when(i == 0)
    def _():
        for p in range(npair):
            vt = vt_ref[p * LANES:(p + 1) * LANES, :]
            for j in range(nwin):
                w_scr[p * nwin + j] = vt[:, j * GRID_W:j * GRID_W + LANES]

    lo = lax.broadcasted_iota(i32, (GRID_W, LANES), 1) < HEAD_DIM

    def scores(tile):
        rs, dr0, kstart, qstart, p = tile
        ls = slice(p * LANES, (p + 1) * LANES)
        qp = q_ref[pl.ds(qstart, GRID_W), ls]
        zero = jnp.zeros_like(qp)
        q2 = jnp.concatenate([jnp.where(lo, qp, zero), jnp.where(lo, zero, qp)], axis=0)
        s_w = lax.dot_general(k_ref[pl.ds(kstart, win), ls], q2, _NT, preferred_element_type=f32)
        s_c = lax.dot_general(kc_ref[:, ls], q2, _NT, preferred_element_type=f32)
        s_w = s_w + jnp.concatenate([bias_ref[p, dr0 + 2 * jj] for jj in range(half)], axis=0)
        return s_w, s_c

    def finish(tile, s_w, s_c):
        rs, dr0, kstart, qstart, p = tile
        ls = slice(p * LANES, (p + 1) * LANES)
        m = jnp.maximum(jnp.max(s_w, axis=0, keepdims=True), jnp.max(s_c, axis=0, keepdims=True))
        e_w = jnp.exp(s_w - m)
        e_c = jnp.exp(s_c - m)
        l = jnp.sum(e_w, axis=0, keepdims=True) + jnp.sum(e_c, axis=0, keepdims=True)
        vtw = jnp.concatenate([w_scr[p * nwin + rs + 2 * jj] for jj in range(half)], axis=1)
        ot = (jnp.dot(vtw, e_w.astype(bf16), preferred_element_type=f32)
              + jnp.dot(vct_ref[ls, :], e_c.astype(bf16), preferred_element_type=f32))
        of = (ot * (1.0 / l)).T
        o_ref[pl.ds(qstart, GRID_W), ls] = jnp.where(lo, of[:GRID_W], of[GRID_W:])

    def rows_body(jb, carry):
        tiles = []
        for u in range(unroll):
            j = jb * unroll + u
            r = i * rb + j
            rs = jnp.clip(r - half, 0, rows - NA_ROWS)
            dr0 = (NA_ROWS - 1) - (r - rs)
            kstart = pl.multiple_of(rs * GRID_W, GRID_W)
            qstart = pl.multiple_of(j * GRID_W, GRID_W)
            tiles += [(rs, dr0, kstart, qstart, p) for p in range(npair)]
        nxt = scores(tiles[0])
        for n, tile in enumerate(tiles):
            cur = nxt
            if n + 1 < len(tiles):
                nxt = scores(tiles[n + 1])
            finish(tile, *cur)
        return carry

    lax.fori_loop(0, rb // unroll, rows_body, 0)


def _bias_pairs_kernel(rv_ref, oh_ref, mask_ref, o_ref):
    acc = None
    for piece in _split3(rv_ref[...]):
        d = jnp.dot(piece, oh_ref[...], preferred_element_type=f32)
        acc = d if acc is None else acc + d
    o_ref[...] = acc + mask_ref[...]


def _na_bias_table(rpb_l):
    nh, ndr, ndc = rpb_l.shape
    pad = 32
    cols = np.arange(GRID_W)
    dc = np.clip(cols[None, :] - cols[:, None], -(NA_COLS - 1), NA_COLS - 1) + (NA_COLS - 1)
    col_start = np.clip(cols - NA_COLS // 2, 0, GRID_W - NA_COLS)
    col_in = (cols[None, :] >= col_start[:, None]) & (cols[None, :] < col_start[:, None] + NA_COLS)
    oh = np.zeros((2, 2, pad, 2, GRID_W, 2, GRID_W), np.float32)
    for par in range(2):
        for s in range(2):
            oh[par, s, dc, s, cols[None, :], par, cols[:, None]] = 1.0
    ncol = 2 * GRID_W * LANES
    oh = jnp.asarray(oh.reshape(4 * pad, ncol), bf16)
    mask = np.where(np.broadcast_to(col_in.T[None, :, None, :], (2, GRID_W, 2, GRID_W)), 0.0, NEG_INF)
    mask = jnp.asarray(mask.reshape(1, ncol), f32)
    rp = jnp.pad(rpb_l, ((0, 0), (0, 0), (0, pad - ndc)))
    rv = jnp.concatenate([rp[:, 0:ndr - 1], rp[:, 1:ndr]], axis=-1)
    rv = rv.reshape(nh // 2, 2, ndr - 1, 2 * pad).transpose(0, 2, 1, 3).reshape(nh // 2 * (ndr - 1), 4 * pad)
    tn = 2048
    out = pl.pallas_call(
        _bias_pairs_kernel, grid=(ncol // tn,),
        in_specs=[pl.BlockSpec(rv.shape, lambda j: (0, 0)),
                  pl.BlockSpec((4 * pad, tn), lambda j: (0, j)),
                  pl.BlockSpec((1, tn), lambda j: (0, j))],
        out_specs=pl.BlockSpec((rv.shape[0], tn), lambda j: (0, j)),
        out_shape=jax.ShapeDtypeStruct((rv.shape[0], ncol), f32),
        compiler_params=_cparams(("parallel",)), name="na_bias_table",
    )(rv, oh, mask)
    return out.reshape(nh // 2, ndr - 1, LANES, LANES)


def _neighbourhood_attention(q, k, vt, kc, vct, bias):
    b, n, w = q.shape
    l = kc.shape[1]
    rows = n // GRID_W
    rb = 8
    return pl.pallas_call(
        functools.partial(_na_kernel, rows=rows, rb=rb),
        grid=(b, rows // rb),
        in_specs=[pl.BlockSpec((None, rb * GRID_W, w), lambda bi, i: (bi, i, 0)),
                  pl.BlockSpec((None, n, w), lambda bi, i: (bi, 0, 0)),
                  pl.BlockSpec((None, w, n), lambda bi, i: (bi, 0, 0)),
                  pl.BlockSpec((None, l, w), lambda bi, i: (bi, 0, 0)),
                  pl.BlockSpec((None, w, l), lambda bi, i: (bi, 0, 0)),
                  pl.BlockSpec(bias.shape, lambda bi, i: (0, 0, 0, 0))],
        out_specs=pl.BlockSpec((None, rb * GRID_W, w), lambda bi, i: (bi, i, 0)),
        out_shape=jax.ShapeDtypeStruct((b, n, w), f32),
        scratch_shapes=[pltpu.VMEM((NA_HEADS // 2 * (rows - 1), LANES, LANES), bf16)],
        compiler_params=_cparams(("parallel", "arbitrary"), 48 * 1024 * 1024),
        name="neighbourhood_attention",
    )(q, k, vt, kc, vct, bias)


def _ctx_attn_kernel(q_ref, k_ref, v_ref, o_ref):
    m = q_ref.shape[0]
    lo = lax.broadcasted_iota(i32, (m, LANES), 1) < HEAD_DIM
    for p in range(NA_HEADS // 2):
        ls = slice(p * LANES, (p + 1) * LANES)
        qp, kp, vp = q_ref[:, ls], k_ref[:, ls], v_ref[:, ls]
        outs = []
        for par in range(2):
            qm = jnp.where(lo if par == 0 else jnp.logical_not(lo), qp, jnp.zeros_like(qp))
            s = lax.dot_general(qm, kp, _NT, preferred_element_type=f32)
            outs.append(_softmax_pv([(s, vp)]))
        o_ref[:, ls] = jnp.where(lo, outs[0], outs[1])


def _context_attention(q, k, v):
    b, l, w = q.shape
    spec = pl.BlockSpec((None, l, w), lambda bi: (bi, 0, 0))
    return pl.pallas_call(
        _ctx_attn_kernel, grid=(b,), in_specs=[spec, spec, spec], out_specs=spec,
        out_shape=jax.ShapeDtypeStruct((b, l, w), f32),
        compiler_params=_cparams(("parallel",)), name="context_attention",
    )(q, k, v)


def _dft_tables_channel():
    c = np.arange(FOURIER_GROUP_DIM)
    ang = 2.0 * np.pi * np.outer(c, c) / FOURIER_GROUP_DIM
    g = FOURIER_WIDTH // FOURIER_GROUP_DIM
    cb = np.kron(np.eye(g), np.cos(ang))
    sb = np.kron(np.eye(g), np.sin(ang))
    return np.concatenate([cb, -sb], axis=1)


def _fourier_kernel(f_ref, cs_ref, d1_ref, d3_ref, tc_ref, ts_ref, o_ref, g_scr, br_scr, bi_scr, o_scr, *, r):
    nt = FOURIER_WIDTH // LANES
    g = jnp.dot(f_ref[...], cs_ref[...].astype(bf16), preferred_element_type=f32)
    for c in range(2 * nt):
        g_scr[c] = g[:, c * LANES:(c + 1) * LANES]
    d1 = d1_ref[...].astype(bf16)

    def strided(scr, tiles, start, size, stride):
        return jnp.concatenate([scr[c, pl.ds(start, size, stride=stride), :] for c in tiles], axis=1)

    for n2 in range(GRID_W):
        gr = strided(g_scr, range(nt), n2, r, GRID_W)
        gi = strided(g_scr, range(nt, 2 * nt), n2, r, GRID_W)
        rhs = jnp.concatenate([gr, gi], axis=0).astype(bf16)
        a = jnp.dot(d1, rhs, preferred_element_type=f32)
        ar, ai = a[:r], a[r:]
        tc = tc_ref[n2 * r:(n2 + 1) * r, :]
        ts = ts_ref[n2 * r:(n2 + 1) * r, :]
        br = ar * tc + ai * ts
        bi = ai * tc - ar * ts
        for c in range(nt):
            br_scr[c, n2 * r:(n2 + 1) * r, :] = br[:, c * LANES:(c + 1) * LANES]
            bi_scr[c, n2 * r:(n2 + 1) * r, :] = bi[:, c * LANES:(c + 1) * LANES]
    d3 = d3_ref[...].astype(bf16)
    for k1 in range(r):
        br = strided(br_scr, range(nt), k1, GRID_W, r)
        bi = strided(bi_scr, range(nt), k1, GRID_W, r)
        rhs = jnp.concatenate([br, bi], axis=0).astype(bf16)
        y = jnp.dot(d3, rhs, preferred_element_type=f32)
        for c in range(nt):
            o_scr[c, pl.ds(k1, GRID_W, stride=r), :] = y[:, c * LANES:(c + 1) * LANES]
    for c in range(nt):
        o_ref[:, c * LANES:(c + 1) * LANES] = o_scr[c]


def _fourier_mix(f):
    b, n, w = f.shape
    r = n // GRID_W
    cs = jnp.asarray(_dft_tables_channel(), f32)
    a1 = 2.0 * np.pi * np.outer(np.arange(r), np.arange(r)) / r
    c1, s1 = np.cos(a1), np.sin(a1)
    d1 = jnp.asarray(np.block([[c1, s1], [-s1, c1]]), f32)
    a3 = 2.0 * np.pi * np.outer(np.arange(GRID_W), np.arange(GRID_W)) / GRID_W
    d3 = jnp.asarray(np.concatenate([np.cos(a3), np.sin(a3)], axis=1), f32)
    n2 = np.arange(GRID_W)[:, None]
    k1 = np.arange(r)[None, :]
    at = 2.0 * np.pi * ((n2 * k1) % n) / n
    scale = 1.0 / math.sqrt(n * FOURIER_GROUP_DIM)
    tc = jnp.asarray(np.broadcast_to((np.cos(at) * scale).reshape(n, 1), (n, w)), f32)
    ts = jnp.asarray(np.broadcast_to((np.sin(at) * scale).reshape(n, 1), (n, w)), f32)
    const = lambda a: pl.BlockSpec(a.shape, lambda bi: (0,) * a.ndim)
    return pl.pallas_call(
        functools.partial(_fourier_kernel, r=r),
        grid=(b,),
        in_specs=[pl.BlockSpec((None, n, w), lambda bi: (bi, 0, 0)),
                  const(cs), const(d1), const(d3), const(tc), const(ts)],
        out_specs=pl.BlockSpec((None, n, w), lambda bi: (bi, 0, 0)),
        out_shape=jax.ShapeDtypeStruct((b, n, w), f32),
        scratch_shapes=[pltpu.VMEM((2 * w // LANES, n, LANES), f32)] + [pltpu.VMEM((w // LANES, n, LANES), f32)] * 3,
        compiler_params=_cparams(("parallel",), 48 * 1024 * 1024),
        name="fourier_mix",
    )(f, cs, d1, d3, tc, ts)


def _fourier_direct_kernel(f_ref, cs_ref, dn_ref, o_ref):
    w = FOURIER_WIDTH
    g = jnp.dot(f_ref[...], cs_ref[...].astype(bf16), preferred_element_type=f32)
    rhs = jnp.concatenate([g[:, :w], g[:, w:]], axis=0).astype(bf16)
    o_ref[...] = jnp.dot(dn_ref[...].astype(bf16), rhs, preferred_element_type=f32)


def _fourier_mix_direct(f):
    b, n, w = f.shape
    cs = jnp.asarray(_dft_tables_channel(), f32)
    a = 2.0 * np.pi * (np.outer(np.arange(n), np.arange(n)) % n) / n
    scale = 1.0 / math.sqrt(n * FOURIER_GROUP_DIM)
    dn = jnp.asarray(np.concatenate([np.cos(a), np.sin(a)], axis=1) * scale, f32)
    const = lambda t: pl.BlockSpec(t.shape, lambda bi: (0,) * t.ndim)
    return pl.pallas_call(
        _fourier_direct_kernel, grid=(b,),
        in_specs=[pl.BlockSpec((None, n, w), lambda bi: (bi, 0, 0)), const(cs), const(dn)],
        out_specs=pl.BlockSpec((None, n, w), lambda bi: (bi, 0, 0)),
        out_shape=jax.ShapeDtypeStruct((b, n, w), f32),
        compiler_params=_cparams(("parallel",)), name="fourier_mix_direct",
    )(f, cs, dn)


def _to_rows(t):
    st = jnp.stack([t[:, c * LANES:(c + 1) * LANES] for c in range(ROW_TILES)], axis=0)
    return jnp.swapaxes(st, 0, 1)


def _from_rows(rows):
    t = jnp.swapaxes(rows, 0, 1)
    return [t[c] for c in range(ROW_TILES)]


def _merge_kernel(na_ref, bg_ref, z_ref, zp_ref, zn_ref, fo_ref, x_ref, cw_ref, gmix_ref, wout_ref,
                  gpost_ref, g1_ref, sh2_ref, sc2_ref, gpre_ref, wr_ref,
                  xo_ref, h2_ref, lg_ref):
    i = pl.program_id(1)
    nt = pl.num_programs(1)
    tm = z_ref.shape[0]
    z = z_ref[...]
    zp = jnp.where(i > 0, zp_ref[SUBLANES - 1:SUBLANES, :], 0.0)
    zn = jnp.where(i < nt - 1, zn_ref[0:1, :], 0.0)
    rid = lax.broadcasted_iota(i32, z.shape, 0)
    z_dn = jnp.where(rid == 0, zp, pltpu.roll(z, 1, axis=0))
    z_up = jnp.where(rid == tm - 1, zn, pltpu.roll(z, tm - 1, axis=0))
    cv = bg_ref[...] * (cw_ref[0:1, :] * z_dn + cw_ref[1:2, :] * z + cw_ref[2:3, :] * z_up)

    a, b_ = NA_WIDTH, NA_WIDTH + CONV_CH
    parts = ((na_ref[...], 0, a), (cv, a, b_), (fo_ref[...], b_, D_MODEL))
    m = None
    for t, lo, hi in parts:
        y = (_rms(t) * gmix_ref[:, lo:hi]).astype(bf16)
        d = jnp.dot(y, wout_ref[lo:hi, :], preferred_element_type=f32)
        m = d if m is None else m + d
    xn = x_ref[...] + g1_ref[...] * (_rms(m) * gpost_ref[...])
    xo_ref[...] = xn
    h2 = (_rms(xn) * gpre_ref[...]) * (1.0 + sc2_ref[...]) + sh2_ref[...]
    h2_ref[...] = _to_rows(h2)
    hh = h2.astype(bf16)
    hl = (h2 - hh.astype(f32)).astype(bf16)
    wr = wr_ref[...]
    wh = wr.astype(bf16)
    wl = (wr - wh.astype(f32)).astype(bf16)
    nt_dot = lambda u, v: lax.dot_general(u, v, _NT, preferred_element_type=f32)
    lg_ref[...] = nt_dot(wh, hh) + (nt_dot(wh, hl) + nt_dot(wl, hh))


def _merge(na, bg, z, fo, x, modl, mod_row, conv_w, g_mix, w_out_b, g_post, g_pre_ffn, w_router_t, tm):
    b, n, d = x.shape
    hb = tm // SUBLANES
    nhb = n // SUBLANES
    row = lambda c: pl.BlockSpec((None, None, 1, d), lambda bi, i: (mod_row(bi), c, 0, 0))
    tok = lambda w: pl.BlockSpec((None, tm, w), lambda bi, i: (bi, i, 0))
    const = lambda a: pl.BlockSpec(a.shape, lambda bi, i: (0,) * a.ndim)
    vec = lambda a: a.reshape(1, -1)
    nt = n // tm
    args = (na, bg, z, z, z, fo, x, conv_w, vec(g_mix), w_out_b, vec(g_post), modl, modl, modl,
            vec(g_pre_ffn), w_router_t)
    return pl.pallas_call(
        _merge_kernel,
        grid=(b, nt),
        in_specs=[tok(NA_WIDTH), tok(CONV_CH), tok(CONV_CH),
                  pl.BlockSpec((None, SUBLANES, CONV_CH), lambda bi, i: (bi, jnp.maximum(i * hb - 1, 0), 0)),
                  pl.BlockSpec((None, SUBLANES, CONV_CH), lambda bi, i: (bi, jnp.minimum((i + 1) * hb, nhb - 1), 0)),
                  tok(FOURIER_WIDTH), tok(d), const(conv_w), const(vec(g_mix)), const(w_out_b),
                  const(vec(g_post)), row(2), row(3), row(4), const(vec(g_pre_ffn)), const(w_router_t)],
        out_specs=[tok(d),
                   pl.BlockSpec((tm, ROW_TILES, LANES), lambda bi, i: (bi * nt + i, 0, 0)),
                   pl.BlockSpec((None, N_EXPERTS, tm), lambda bi, i: (bi, 0, i))],
        out_shape=[jax.ShapeDtypeStruct((b, n, d), f32),
                   jax.ShapeDtypeStruct((b * n, ROW_TILES, LANES), f32),
                   jax.ShapeDtypeStruct((b, N_EXPERTS, n), f32)],
        compiler_params=_cparams(("parallel", "parallel"), 40 * 1024 * 1024),
        name="merge",
    )(*args)


def _split3(a):
    h = a.astype(bf16)
    r1 = a - h.astype(f32)
    m = r1.astype(bf16)
    l = (r1 - m.astype(f32)).astype(bf16)
    return h, m, l


def _threshold_search(aff, count, cap):
    t = jnp.zeros(aff.shape, i32)
    for lo_bit in range(27, -1, -3):
        digit = jnp.zeros(aff.shape, i32)
        for j in range(1, 8):
            cand = lax.bitcast_convert_type(t | (j << lo_bit), f32)
            digit = digit + jnp.where(count(aff >= cand) >= cap, 1, 0)
        t = t | (digit << lo_bit)
    return lax.bitcast_convert_type(t, f32)


def _route_kernel(lg_ref, o_ref, *, nch, cap):
    ne = N_EXPERTS
    r = ne * nch
    sh = nch.bit_length() - 1
    lg = lg_ref[...]
    ex = jnp.exp(lg - jnp.max(lg, axis=0, keepdims=True))
    aff = (ex / jnp.sum(ex, axis=0, keepdims=True)).reshape(r, LANES)

    li = lax.broadcasted_iota(i32, (LANES, LANES), 0)
    lj = lax.broadcasted_iota(i32, (LANES, LANES), 1)
    triu = _ones_where(li <= lj)
    tril = _ones_where(lj <= li)
    eye = _ones_where(li == lj)
    ones = jnp.ones((LANES, LANES), bf16)
    ri = lax.broadcasted_iota(i32, (r, r), 0)
    rj = lax.broadcasted_iota(i32, (r, r), 1)
    same = (ri >> sh) == (rj >> sh)
    sl =_ones_where(same & (rj < ri))
    mm = lambda u, v: jnp.dot(u, v, preferred_element_type=f32)

    vpe = nch // SUBLANES
    e8i = lax.broadcasted_iota(i32, (ne * SUBLANES, ne * SUBLANES), 0) >> 3
    e8j = lax.broadcasted_iota(i32, (ne * SUBLANES, ne * SUBLANES), 1) >> 3
    bd8 = _ones_where(e8i == e8j)

    def count(mask):
        m = jnp.where(mask, 1.0, 0.0).reshape(ne, vpe, SUBLANES, LANES)
        m = jnp.sum(m, axis=1).reshape(ne * SUBLANES, LANES).astype(bf16)
        tot = mm(mm(bd8, m).astype(bf16), ones)
        tot = jnp.broadcast_to(tot.reshape(ne, 1, SUBLANES, LANES), (ne, vpe, SUBLANES, LANES))
        return tot.reshape(r, LANES)

    thr = _threshold_search(aff, count, cap)
    gt = aff > thr
    eq = aff == thr
    need = cap - count(gt)
    eqb = _ones_where(eq)
    pe = mm(eqb, triu) + mm(sl, mm(eqb, ones).astype(bf16))
    sel = gt | (eq & (pe <= need))
    selb = _ones_where(sel)
    rt = mm(selb, ones)
    oin = mm(sl, rt.astype(bf16)) + rt

    reps = cap // LANES
    tile = lambda a: jnp.concatenate([a] * reps, axis=1) if reps > 1 else a
    s_f = lax.broadcasted_iota(i32, (nch, cap), 1).astype(f32)
    s8 = lax.broadcasted_iota(i32, (SUBLANES, cap), 1).astype(f32)
    ji = lax.broadcasted_iota(i32, (nch, cap), 0).astype(f32)
    lis = lax.broadcasted_iota(i32, (LANES, cap), 0).astype(f32)
    ri8 = lax.broadcasted_iota(i32, (SUBLANES, cap), 0)
    ones8c = jnp.ones((SUBLANES, nch), bf16)
    ones8l = jnp.ones((SUBLANES, LANES), bf16)
    for e in range(ne):
        rows = slice(e * nch, (e + 1) * nch)
        zmask = tile(oin[rows]) <= s_f
        js = mm(ones8c, _ones_where(zmask))
        oex = mm(ones8c, jnp.where(zmask, tile(rt[rows]), 0.0).astype(bf16))
        s_loc = s8 - oex
        u = _ones_where(ji == jnp.broadcast_to(js[0:1], (nch, cap)))
        plt = lax.dot_general(tril, selb[rows], _NT, preferred_element_type=f32)
        ptg = mm(plt.astype(bf16), u)
        c = _ones_where(ptg <= jnp.broadcast_to(s_loc[0:1], (LANES, cap)))
        tl = mm(ones8l, c)
        idx = js * float(LANES) + tl
        affg = None
        for piece in _split3(aff[rows]):
            pt = lax.dot_general(eye, piece, _NT, preferred_element_type=f32).astype(bf16)
            pg = mm(pt, u)
            affg = pg if affg is None else affg + pg
        hit = lis == jnp.broadcast_to(tl[0:1], (LANES, cap))
        gate = jnp.sum(jnp.where(hit, affg, 0.0), axis=0, keepdims=True)
        o_ref[e] = jnp.where(ri8 == 0, idx, jnp.where(ri8 == 1, jnp.broadcast_to(gate, (SUBLANES, cap)), 0.0))


def _route_small_kernel(lg_ref, o_ref, *, cap):
    ne, n = lg_ref.shape
    lg = lg_ref[...]
    ex = jnp.exp(lg - jnp.max(lg, axis=0, keepdims=True))
    aff = ex / jnp.sum(ex, axis=0, keepdims=True)
    li = lax.broadcasted_iota(i32, (n, n), 0)
    lj = lax.broadcasted_iota(i32, (n, n), 1)
    triu = _ones_where(li <= lj)
    ones = jnp.ones((n, n), bf16)
    mm = lambda u, v: jnp.dot(u, v, preferred_element_type=f32)
    count = lambda mask: mm(_ones_where(mask), ones)
    thr = _threshold_search(aff, count, cap)
    gt = aff > thr
    eq = aff == thr
    need = cap - count(gt)
    sel = gt | (eq & (mm(_ones_where(eq), triu) <= need))
    slot = jnp.where(sel, mm(_ones_where(sel), triu) - 1.0, -1.0)
    tid = lax.broadcasted_iota(i32, (SUBLANES, n), 1).astype(f32)
    ri = lax.broadcasted_iota(i32, (SUBLANES, n), 0)
    s_i = lax.broadcasted_iota(i32, (cap, n), 0).astype(f32)
    ro = lax.broadcasted_iota(i32, (SUBLANES, cap), 0)
    bc = lambda a: jnp.broadcast_to(a, (SUBLANES, n))
    for e in range(ne):
        et = _ones_where(s_i == jnp.broadcast_to(slot[e:e + 1], (cap, n)))
        h, m, l = _split3(aff[e:e + 1])
        rm = jnp.where(ri == 0, tid, jnp.where(ri == 1, bc(h.astype(f32)), jnp.where(
            ri == 2, bc(m.astype(f32)), jnp.where(ri == 3, bc(l.astype(f32)), 0.0)))).astype(bf16)
        res = lax.dot_general(rm, et, _NT, preferred_element_type=f32)
        gate = res[1:2] + res[2:3] + res[3:4]
        o_ref[e] = jnp.where(ro == 0, jnp.broadcast_to(res[0:1], (SUBLANES, cap)),
                             jnp.where(ro == 1, jnp.broadcast_to(gate, (SUBLANES, cap)), 0.0))


def _route(logits_t):
    b, ne, n = logits_t.shape
    cap = CAPACITY_FACTOR * n // N_EXPERTS
    out_shape = jax.ShapeDtypeStruct((b, ne, SUBLANES, cap), f32)
    out_spec = pl.BlockSpec((None, ne, SUBLANES, cap), lambda bi: (bi, 0, 0, 0))
    nch = n // LANES
    if nch % SUBLANES == 0 and cap % LANES == 0:
        res = pl.pallas_call(
            functools.partial(_route_kernel, nch=nch, cap=cap), grid=(b,),
            in_specs=[pl.BlockSpec((None, ne, nch, LANES), lambda bi: (bi, 0, 0, 0))],
            out_specs=out_spec, out_shape=out_shape,
            compiler_params=_cparams(("parallel",), 40 * 1024 * 1024), name="route",
        )(logits_t.reshape(b, ne, nch, LANES))
    else:
        res = pl.pallas_call(
            functools.partial(_route_small_kernel, cap=cap), grid=(b,),
            in_specs=[pl.BlockSpec((None, ne, n), lambda bi: (bi, 0, 0))],
            out_specs=out_spec, out_shape=out_shape,
            compiler_params=_cparams(("parallel",)), name="route_small",
        )(logits_t)
    idx = res[:, :, 0, :].astype(i32).reshape(b * ne, 1, cap)
    gates = res[:, :, 1, :].reshape(b * ne, 1, cap)
    return idx, gates


def _dispatch_kernel(idx_ref, h_ref, o_ref, scr):
    cap = o_ref.shape[0]

    def body(i, c):
        scr[i] = h_ref[idx_ref[0, 0, i]]
        return c

    lax.fori_loop(0, cap, body, 0, unroll=8)
    tiles = _from_rows(scr[...])
    for c in range(ROW_TILES):
        o_ref[:, c * LANES:(c + 1) * LANES] = tiles[c].astype(bf16)


def _dispatch(h2_rows, idx, b):
    n = h2_rows.shape[0] // b
    cap = idx.shape[2]
    ne = N_EXPERTS
    return pl.pallas_call(
        _dispatch_kernel,
        grid=(b, ne),
        in_specs=[pl.BlockSpec((1, 1, cap), lambda bi, e: (bi * ne + e, 0, 0), memory_space=pltpu.SMEM),
                  pl.BlockSpec((n, ROW_TILES, LANES), lambda bi, e: (bi, 0, 0))],
        out_specs=pl.BlockSpec((None, cap, D_MODEL), lambda bi, e: (e, bi, 0)),
        out_shape=jax.ShapeDtypeStruct((ne, b * cap, D_MODEL), bf16),
        scratch_shapes=[pltpu.VMEM((cap, ROW_TILES, LANES), f32)],
        compiler_params=_cparams(("parallel", "arbitrary"), 48 * 1024 * 1024),
        name="dispatch",
    )(idx, h2_rows)


def _combine_kernel(idx_ref, g_ref, y_ref, x_ref, g2_ref, gp_ref, o_ref, acc, ys, *, chunk):
    s = pl.program_id(1)
    ne = N_EXPERTS
    cap = y_ref.shape[0]

    @pl.when(s == 0)
    def _():
        acc[...] = jnp.zeros_like(acc)

    @pl.when(s < ne)
    def _():
        ys[...] = _to_rows(y_ref[...].astype(f32))

        group = 8

        def body(gi, c):
            base = gi * group
            ts = [idx_ref[0, 0, base + k] for k in range(group)]
            vals = [acc[ts[k]] + g_ref[0, 0, base + k] * ys[base + k] for k in range(group)]
            for k in range(group):
                acc[ts[k]] = vals[k]
            return c

        lax.fori_loop(0, cap // group, body, 0, unroll=2)

    @pl.when(s >= ne)
    def _():
        start = pl.multiple_of((s - ne) * chunk, chunk)
        tiles = _from_rows(acc[pl.ds(start, chunk)])
        ss = None
        for t in tiles:
            q = jnp.sum(t * t, axis=-1, keepdims=True)
            ss = q if ss is None else ss + q
        inv = lax.rsqrt(ss * (1.0 / D_MODEL) + RMS_EPS)
        for c in range(ROW_TILES):
            cs = slice(c * LANES, (c + 1) * LANES)
            o_ref[:, cs] = x_ref[:, cs] + g2_ref[:, cs] * ((tiles[c] * inv) * gp_ref[:, cs])


def _combine(y, idx, gates, x, modl, mod_row, g_post):
    b, n, d = x.shape
    ne = N_EXPERTS
    cap = idx.shape[2]
    chunk = min(n, 512)
    nchunk = n // chunk
    last = ne - 1
    sm = lambda: pl.BlockSpec((1, 1, cap), lambda bi, s: (bi * ne + jnp.minimum(s, last), 0, 0),
                              memory_space=pltpu.SMEM)
    return pl.pallas_call(
        functools.partial(_combine_kernel, chunk=chunk),
        grid=(b, ne + nchunk),
        in_specs=[sm(), sm(),
                  pl.BlockSpec((None, cap, D_MODEL), lambda bi, s: (jnp.minimum(s, last), bi, 0)),
                  pl.BlockSpec((None, chunk, d), lambda bi, s: (bi, jnp.maximum(s - ne, 0), 0)),
                  pl.BlockSpec((None, None, 1, d), lambda bi, s: (mod_row(bi), N_MOD - 1, 0, 0)),
                  pl.BlockSpec((1, d), lambda bi, s: (0, 0))],
        out_specs=pl.BlockSpec((None, chunk, D_MODEL), lambda bi, s: (bi, jnp.maximum(s - ne, 0), 0)),
        out_shape=jax.ShapeDtypeStruct((b, n, D_MODEL), f32),
        scratch_shapes=[pltpu.VMEM((n, ROW_TILES, LANES), f32), pltpu.VMEM((cap, ROW_TILES, LANES), f32)],
        compiler_params=_cparams(("parallel", "arbitrary"), 48 * 1024 * 1024),
        name="combine",
    )(idx, gates, y, x, modl, g_post.reshape(1, d))


MXU_WIDTH = 256


def _gate_up_kernel(*refs, nseg, tf):
    xs = refs[:nseg]
    wg_ref, wu_ref = refs[nseg:nseg + 2]
    hs = refs[nseg + 2:]
    pieces = [(a, min(a + MXU_WIDTH, tf)) for a in range(0, tf, MXU_WIDTH)]
    for x_ref, h_ref in zip(xs, hs):
        x = x_ref[...]
        for a, b in pieces:
            if b - a == MXU_WIDTH:
                g = jnp.dot(x, wg_ref[:, a:b].astype(bf16), preferred_element_type=f32)
                u = jnp.dot(x, wu_ref[:, a:b].astype(bf16), preferred_element_type=f32)
            else:
                w = jnp.concatenate([wg_ref[:, a:b], wu_ref[:, a:b]], axis=1).astype(bf16)
                gu = jnp.dot(x, w, preferred_element_type=f32)
                g, u = gu[:, :b - a], gu[:, b - a:]
            h_ref[:, a:b] = ((g * jax.nn.sigmoid(g)) * u).astype(bf16)


def _down_kernel(*refs, nseg):
    hs = refs[:nseg]
    wd_ref = refs[nseg]
    ys = refs[nseg + 1:]
    wd = wd_ref[...].astype(bf16)
    for h_ref, y_ref in zip(hs, ys):
        y_ref[...] = jnp.dot(h_ref[...], wd, preferred_element_type=f32)


def _expert_ffn(xs_list, w_gate, w_up, w_down, layer):
    _, ne, d, fh = w_gate.shape
    tf = fh // 2
    tn = d // 2
    assert tf % LANES == 0 and tn % LANES == 0
    nseg = len(xs_list)
    ms = [x.shape[1] for x in xs_list]
    wspec = pl.BlockSpec((None, None, d, tf), lambda e, s: (layer, e, 0, s))
    hidden = pl.pallas_call(
        functools.partial(_gate_up_kernel, nseg=nseg, tf=tf),
        grid=(ne, fh // tf),
        in_specs=[pl.BlockSpec((None, m, d), lambda e, s: (e, 0, 0)) for m in ms] + [wspec, wspec],
        out_specs=[pl.BlockSpec((None, m, tf), lambda e, s: (e, 0, s)) for m in ms],
        out_shape=[jax.ShapeDtypeStruct((ne, m, fh), bf16) for m in ms],
        compiler_params=_cparams(("parallel", "arbitrary"), VMEM_LIMIT),
        name="expert_gate_up",
    )(*xs_list, w_gate, w_up)
    return pl.pallas_call(
        functools.partial(_down_kernel, nseg=nseg),
        grid=(ne, d // tn),
        in_specs=[pl.BlockSpec((None, m, fh), lambda e, s: (e, 0, 0)) for m in ms]
        + [pl.BlockSpec((None, None, fh, tn), lambda e, s: (layer, e, 0, s))],
        out_specs=[pl.BlockSpec((None, m, tn), lambda e, s: (e, 0, s)) for m in ms],
        out_shape=[jax.ShapeDtypeStruct((ne, m, d), f32) for m in ms],
        compiler_params=_cparams(("parallel", "arbitrary"), VMEM_LIMIT),
        name="expert_down",
    )(*hidden, w_down)


def kernel(x, c, ctx, c_ctx, w_mod, b_mod, g_pre_mix, g_post_mix, g_pre_ffn, g_post_ffn, w_in, rpb,
           conv_w, g_mix_out, w_out, w_router, w_gate, w_up, w_down):
    depth = w_mod.shape[0]
    b, n, d = x.shape
    l_ctx = ctx.shape[1]
    ctx_row = b
    cvec = jnp.concatenate([c, c_ctx[None, :], jnp.zeros((SUBLANES - b - 1, d), f32)], axis=0)
    mod = _modulation(cvec, w_mod, b_mod)
    lat_row = lambda bi: bi
    cx_row = lambda bi: ctx_row
    tm = min(n, 512)
    for layer in range(depth):
        last = layer == depth - 1
        modl = mod[layer].reshape(SUBLANES, N_MOD, 1, d)
        w_in_b = w_in[layer].astype(bf16)
        w_out_b = w_out[layer].astype(bf16)
        w_router_t = w_router[layer].T
        q, k, vt, bg, z, f = _in_projection(x, modl, lat_row, g_pre_mix[layer], w_in_b, tm, False)
        qc, kc, vct, bgc, zc, fc, vc = _in_projection(ctx, modl, cx_row, g_pre_mix[layer], w_in_b, l_ctx, True)
        na = _neighbourhood_attention(q, k, vt, kc, vct, _na_bias_table(rpb[layer]))
        fo = _fourier_mix(f)
        x, h2_rows, logits = _merge(na, bg, z, fo, x, modl, lat_row, conv_w[layer], g_mix_out[layer],
                                    w_out_b, g_post_mix[layer], g_pre_ffn[layer], w_router_t, tm)
        idx, gates = _route(logits)
        xs_list = [_dispatch(h2_rows, idx, b)]
        if not last:
            na_c = _context_attention(qc, kc, vc)
            fo_c = _fourier_mix_direct(fc)
            ctx, h2c_rows, logits_c = _merge(na_c, bgc, zc, fo_c, ctx, modl, cx_row, conv_w[layer],
                                             g_mix_out[layer], w_out_b, g_post_mix[layer],
                                             g_pre_ffn[layer], w_router_t, l_ctx)
            idx_c, gates_c = _route(logits_c)
            xs_list.append(_dispatch(h2c_rows, idx_c, b))
        ys = _expert_ffn(xs_list, w_gate, w_up, w_down, layer)
        x = _combine(ys[0], idx, gates, x, modl, lat_row, g_post_ffn[layer])
        if not last:
            ctx = _combine(ys[1], idx_c, gates_c, ctx, modl, cx_row, g_post_ffn[layer])
    return x
```
